```python
import math
import jax, jax.numpy as jnp
from jax import lax
import numpy as np

D_MODEL = 1024
BATCH = 2
SEQ = 8192
DEPTH = 1

SSD_HEADDIM = 64
SSD_HEADS = 24
SSD_INNER = SSD_HEADS * SSD_HEADDIM
SSD_GROUPS = 4
SSD_STATE = 128
SSD_CONV = 5
SSD_CHUNK = 128
SSD_BC = 2 * SSD_GROUPS * SSD_STATE
XBC_DIM = SSD_INNER + SSD_BC
S5_GROUP = 16
S5_WIDTH = 768
S5_GROUPS = S5_WIDTH // S5_GROUP
S5_STATE = 64
N_BRANCHES = 2
IN_COLS = SSD_INNER + XBC_DIM + 2 * SSD_HEADS + S5_WIDTH + N_BRANCHES * D_MODEL
SPLITS = (SSD_INNER, SSD_INNER + XBC_DIM, SSD_INNER + XBC_DIM + 2 * SSD_HEADS,
          SSD_INNER + XBC_DIM + 2 * SSD_HEADS + S5_WIDTH)
N_EXPERTS = 16
EXPERT_FF = 2816
CAPACITY_FACTOR = 2
DEEPNORM_ALPHA = (2.0 * DEPTH) ** 0.25
DEEPNORM_BETA = (8.0 * DEPTH) ** -0.25
LN_EPS = 1e-5
RMS_EPS = 1e-5
F32 = jnp.float32

kernel_name = "hybrid_ssd_s5_ec_moe_deepnorm"


def _layer_norm(x, g, b):
    xf = x.astype(F32)
    mu = jnp.mean(xf, -1, keepdims=True)
    var = jnp.mean(jnp.square(xf - mu), -1, keepdims=True)
    return ((xf - mu) * lax.rsqrt(var + LN_EPS) * g.astype(F32) + b.astype(F32)).astype(x.dtype)


def _dwconv_centred(x, w, b):
    k = w.shape[0]
    y = lax.conv_general_dilated(x, w[:, None, :].astype(x.dtype), window_strides=(1,),
                                 padding=[(k // 2, k // 2)],
                                 dimension_numbers=('NWC', 'WIO', 'NWC'),
                                 feature_group_count=x.shape[-1])
    return y + b.astype(x.dtype)


def _ssd_scan(xh, dt, a, bm, cm):
    bsz, l, h, p = xh.shape
    g, n = bm.shape[-2:]
    hg = h // g
    q = SSD_CHUNK
    c = l // q
    xdt = (xh.astype(F32) * dt[..., None]).reshape(bsz, c, q, g, hg, p)
    bc = bm.astype(F32).reshape(bsz, c, q, g, n)
    cc = cm.astype(F32).reshape(bsz, c, q, g, n)
    cs = jnp.cumsum((dt * a).reshape(bsz, c, q, g, hg), axis=2)
    seg = cs[:, :, :, None] - cs[:, :, None, :]
    mask = jnp.tril(jnp.ones((q, q), bool))[None, None, :, :, None, None]
    decay = jnp.where(mask, jnp.exp(jnp.where(mask, seg, 0.0)), 0.0)
    cb = jnp.einsum('bcqgn,bcsgn->bcqsg', cc, bc)
    wts = cb[..., None] * decay
    y_diag = jnp.einsum('bcqsgh,bcsghp->bcqghp', wts, xdt)
    last = cs[:, :, -1:]
    xw = xdt * jnp.exp(last - cs)[..., None]
    states = jnp.einsum('bcsgn,bcsghp->bcghpn', bc, xw)
    chunk_decay = jnp.exp(last[:, :, 0])

    def step(carry, inp):
        st, dec = inp
        return carry * dec[..., None, None] + st, carry

    init = jnp.zeros((bsz, g, hg, p, n), F32)
    _, prev = lax.scan(step, init, (jnp.moveaxis(states, 1, 0), jnp.moveaxis(chunk_decay, 1, 0)))
    prev = jnp.moveaxis(prev, 0, 1)
    y_off = jnp.einsum('bcqgn,bcghpn->bcqghp', cc, prev) * jnp.exp(cs)[..., None]
    return (y_diag + y_off).reshape(bsz, l, h, p)


def _ssd_branch(z, xbc, dt_raw, conv_w, conv_b, a_log_f, a_log_b, dt_bias_f, dt_bias_b,
                d_skip, norm_g, w_out):
    bsz, l, _ = z.shape
    xbc = jax.nn.silu(_dwconv_centred(xbc, conv_w, conv_b))
    xs, bm, cm = jnp.split(xbc, [SSD_INNER, SSD_INNER + SSD_BC // 2], axis=-1)
    xh = xs.reshape(bsz, l, SSD_HEADS, SSD_HEADDIM)
    bm = bm.reshape(bsz, l, SSD_GROUPS, SSD_STATE)
    cm = cm.reshape(bsz, l, SSD_GROUPS, SSD_STATE)
    dt_f_raw, dt_b_raw = jnp.split(dt_raw.astype(F32), 2, axis=-1)
    dt_f = jax.nn.softplus(dt_f_raw + dt_bias_f.astype(F32))
    dt_b = jax.nn.softplus(dt_b_raw + dt_bias_b.astype(F32))
    a_f = -jnp.exp(a_log_f.astype(F32))
    a_b = -jnp.exp(a_log_b.astype(F32))
    flip = lambda t: jnp.flip(t, axis=1)
    y_f = _ssd_scan(xh, dt_f, a_f, bm, cm)
    y_b = flip(_ssd_scan(flip(xh), flip(dt_b), a_b, flip(bm), flip(cm)))
    y = y_f + y_b + xh.astype(F32) * d_skip.astype(F32)[:, None]
    y = y.reshape(bsz, l, SSD_INNER) * jax.nn.silu(z.astype(F32))
    y = y.reshape(bsz, l, SSD_GROUPS, SSD_INNER // SSD_GROUPS)
    y = y * lax.rsqrt(jnp.mean(y * y, -1, keepdims=True) + RMS_EPS)
    y = y.reshape(bsz, l, SSD_INNER) * norm_g.astype(F32)
    return y.astype(z.dtype) @ w_out


def _s5_direction(ug, a_re, a_im, log_step, b_re, b_im):
    lam = lax.complex(a_re.astype(F32), a_im.astype(F32))
    step = jnp.exp(log_step.astype(F32))[:, None]
    a_bar = jnp.exp(lam * step)
    b_bar = ((a_bar - 1.0) / lam)[..., None] * lax.complex(b_re.astype(F32), b_im.astype(F32))
    bu = jnp.einsum('blgk,gpk->blgp', ug.astype(F32).astype(jnp.complex64), b_bar)
    a_seq = jnp.broadcast_to(a_bar, (1, ug.shape[1]) + a_bar.shape)

    def combine(e1, e2):
        a1, b1 = e1
        a2, b2 = e2
        return a2 * a1, a2 * b1 + b2

    _, states = lax.associative_scan(combine, (a_seq, bu), axis=1)
    return states


def _s5_branch(u, a_re_f, a_im_f, log_step_f, a_re_b, a_im_b, log_step_b,
               b_re, b_im, c_re, c_im, d_skip, w_glu_a, w_glu_b):
    bsz, l, _ = u.shape
    ug = u.reshape(bsz, l, S5_GROUPS, S5_GROUP)
    s_f = _s5_direction(ug, a_re_f, a_im_f, log_step_f, b_re, b_im)
    s_b = jnp.flip(_s5_direction(jnp.flip(ug, 1), a_re_b, a_im_b, log_step_b, b_re, b_im), 1)
    c = lax.complex(c_re.astype(F32), c_im.astype(F32))
    y = jnp.real(jnp.einsum('blgp,gkp->blgk', s_f + s_b, c)).reshape(bsz, l, S5_WIDTH)
    y = y + u.astype(F32) * d_skip.astype(F32)
    y = jax.nn.gelu(y).astype(u.dtype)
    return (y @ w_glu_a) * jax.nn.sigmoid(y @ w_glu_b)


def _expert_choice_ffn(x, w_router, w_e1, w_e3, w_e2):
    bsz, n, d = x.shape
    cap = CAPACITY_FACTOR * n // N_EXPERTS
    aff = jax.nn.softmax((x @ w_router).astype(F32), axis=-1)
    gate, idx = lax.top_k(jnp.swapaxes(aff, 1, 2), cap)
    xin = jax.vmap(lambda xb, ib: xb[ib])(x, idx)
    hid = jax.nn.silu(jnp.einsum('becd,edf->becf', xin, w_e1)) * jnp.einsum('becd,edf->becf', xin, w_e3)
    y = jnp.einsum('becf,efd->becd', hid, w_e2) * gate[..., None].astype(x.dtype)
    flat_idx = (idx + (jnp.arange(bsz, dtype=idx.dtype) * n)[:, None, None]).reshape(-1)
    out = jax.ops.segment_sum(y.reshape(-1, d), flat_idx, num_segments=bsz * n)
    return out.reshape(bsz, n, d)


def _layer(h, w_in, conv_w, conv_b, ssd_a_log_f, ssd_a_log_b, ssd_dt_bias_f, ssd_dt_bias_b,
           ssd_d, ssd_norm_g, w_out_ssd, s5_a_re_f, s5_a_im_f, s5_log_step_f, s5_a_re_b,
           s5_a_im_b, s5_log_step_b, s5_b_re, s5_b_im, s5_c_re, s5_c_im, s5_d, w_glu_a, w_glu_b,
           w_o, ln1_g, ln1_b, w_router, w_e1, w_e3, w_e2, ln2_g, ln2_b):
    proj = h @ w_in
    z, xbc, dt_raw, u, gates = jnp.split(proj, SPLITS, axis=-1)
    y_a = _ssd_branch(z, xbc, dt_raw, conv_w, conv_b, ssd_a_log_f, ssd_a_log_b,
                      ssd_dt_bias_f, ssd_dt_bias_b, ssd_d, ssd_norm_g, w_out_ssd)
    y_b = _s5_branch(u, s5_a_re_f, s5_a_im_f, s5_log_step_f, s5_a_re_b, s5_a_im_b, s5_log_step_b,
                     s5_b_re, s5_b_im, s5_c_re, s5_c_im, s5_d, w_glu_a, w_glu_b)
    g_a, g_b = jnp.split(gates, 2, axis=-1)
    mix = jax.nn.sigmoid(g_a) * y_a + jax.nn.sigmoid(g_b) * y_b
    h = _layer_norm(DEEPNORM_ALPHA * h + mix @ w_o, ln1_g, ln1_b)
    h = _layer_norm(DEEPNORM_ALPHA * h + _expert_choice_ffn(h, w_router, w_e1, w_e3, w_e2), ln2_g, ln2_b)
    return h


def setup_inputs(seed: int = 0) -> dict:
    key = jax.random.key(seed)
    ks = iter(jax.random.split(key, 48))
    L_ = DEPTH

    def nrm(shape, scale):
        return jax.random.normal(next(ks), (L_,) + shape, F32) * scale

    def unif(shape, lo, hi):
        return jax.random.uniform(next(ks), (L_,) + shape, F32, lo, hi)

    def dt_bias():
        dt = jnp.exp(unif((SSD_HEADS,), math.log(1e-3), math.log(1e-1)))
        return dt + jnp.log(-jnp.expm1(-dt))

    a_im_base = jnp.pi * jnp.arange(S5_STATE, dtype=F32)
    inp = {}
    inp["x"] = jax.random.normal(next(ks), (BATCH, SEQ, D_MODEL), F32)
    inp["w_in"] = nrm((D_MODEL, IN_COLS), D_MODEL ** -0.5)
    inp["conv_w"] = nrm((SSD_CONV, XBC_DIM), SSD_CONV ** -0.5)
    inp["conv_b"] = nrm((XBC_DIM,), 0.01)
    inp["ssd_a_log_f"] = jnp.log(unif((SSD_HEADS,), 1.0, 16.0))
    inp["ssd_a_log_b"] = jnp.log(unif((SSD_HEADS,), 1.0, 16.0))
    inp["ssd_dt_bias_f"] = dt_bias()
    inp["ssd_dt_bias_b"] = dt_bias()
    inp["ssd_d"] = 1.0 + nrm((SSD_HEADS,), 0.1)
    inp["ssd_norm_g"] = 1.0 + nrm((SSD_INNER,), 0.02)
    inp["w_out_ssd"] = nrm((SSD_INNER, D_MODEL), SSD_INNER ** -0.5)
    inp["s5_a_re_f"] = -0.5 + nrm((S5_GROUPS, S5_STATE), 0.01)
    inp["s5_a_im_f"] = a_im_base + nrm((S5_GROUPS, S5_STATE), 0.01)
    inp["s5_log_step_f"] = unif((S5_GROUPS,), math.log(1e-3), math.log(1e-1))
    inp["s5_a_re_b"] = -0.5 + nrm((S5_GROUPS, S5_STATE), 0.01)
    inp["s5_a_im_b"] = a_im_base + nrm((S5_GROUPS, S5_STATE), 0.01)
    inp["s5_log_step_b"] = unif((S5_GROUPS,), math.log(1e-3), math.log(1e-1))
    inp["s5_b_re"] = nrm((S5_GROUPS, S5_STATE, S5_GROUP), (2 * S5_GROUP) ** -0.5)
    inp["s5_b_im"] = nrm((S5_GROUPS, S5_STATE, S5_GROUP), (2 * S5_GROUP) ** -0.5)
    inp["s5_c_re"] = nrm((S5_GROUPS, S5_GROUP, S5_STATE), S5_STATE ** -0.5)
    inp["s5_c_im"] = nrm((S5_GROUPS, S5_GROUP, S5_STATE), S5_STATE ** -0.5)
    inp["s5_d"] = nrm((S5_WIDTH,), 1.0)
    inp["w_glu_a"] = nrm((S5_WIDTH, D_MODEL), S5_WIDTH ** -0.5)
    inp["w_glu_b"] = nrm((S5_WIDTH, D_MODEL), S5_WIDTH ** -0.5)
    inp["w_o"] = nrm((D_MODEL, D_MODEL), DEEPNORM_BETA * D_MODEL ** -0.5)
    inp["ln1_g"] = 1.0 + nrm((D_MODEL,), 0.02)
    inp["ln1_b"] = nrm((D_MODEL,), 0.02)
    inp["w_router"] = nrm((D_MODEL, N_EXPERTS), D_MODEL ** -0.5)
    inp["w_e1"] = nrm((N_EXPERTS, D_MODEL, EXPERT_FF), D_MODEL ** -0.5)
    inp["w_e3"] = nrm((N_EXPERTS, D_MODEL, EXPERT_FF), D_MODEL ** -0.5)
    inp["w_e2"] = nrm((N_EXPERTS, EXPERT_FF, D_MODEL), DEEPNORM_BETA * EXPERT_FF ** -0.5)
    inp["ln2_g"] = 1.0 + nrm((D_MODEL,), 0.02)
    inp["ln2_b"] = nrm((D_MODEL,), 0.02)
    return inp


def reference(x, w_in, conv_w, conv_b, ssd_a_log_f, ssd_a_log_b, ssd_dt_bias_f, ssd_dt_bias_b,
              ssd_d, ssd_norm_g, w_out_ssd, s5_a_re_f, s5_a_im_f, s5_log_step_f, s5_a_re_b,
              s5_a_im_b, s5_log_step_b, s5_b_re, s5_b_im, s5_c_re, s5_c_im, s5_d, w_glu_a, w_glu_b,
              w_o, ln1_g, ln1_b, w_router, w_e1, w_e3, w_e2, ln2_g, ln2_b):
    h = x
    for i in range(DEPTH):
        h = _layer(h, w_in[i], conv_w[i], conv_b[i], ssd_a_log_f[i], ssd_a_log_b[i],
                   ssd_dt_bias_f[i], ssd_dt_bias_b[i], ssd_d[i], ssd_norm_g[i], w_out_ssd[i],
                   s5_a_re_f[i], s5_a_im_f[i], s5_log_step_f[i], s5_a_re_b[i], s5_a_im_b[i],
                   s5_log_step_b[i], s5_b_re[i], s5_b_im[i], s5_c_re[i], s5_c_im[i], s5_d[i],
                   w_glu_a[i], w_glu_b[i], w_o[i], ln1_g[i], ln1_b[i], w_router[i], w_e1[i],
                   w_e3[i], w_e2[i], ln2_g[i], ln2_b[i])
    return h
```

```python
import functools
import math

import jax
import jax.numpy as jnp
from jax import lax
from jax.experimental import pallas as pl
from jax.experimental.pallas import tpu as pltpu

F32 = jnp.float32
BF16 = jnp.bfloat16
I32 = jnp.int32

D_MODEL = 1024
SSD_HEADDIM = 64
SSD_HEADS = 24
SSD_INNER = SSD_HEADS * SSD_HEADDIM
SSD_GROUPS = 4
SSD_STATE = 128
SSD_CONV = 5
SSD_BC = 2 * SSD_GROUPS * SSD_STATE
XBC_DIM = SSD_INNER + SSD_BC
S5_GROUP = 16
S5_WIDTH = 768
S5_GROUPS = S5_WIDTH // S5_GROUP
S5_STATE = 64
N_EXPERTS = 16
EXPERT_FF = 2816
CAPACITY_FACTOR = 2
DEPTH = 1
DEEPNORM_ALPHA = (2.0 * DEPTH) ** 0.25
LN_EPS = 1e-5
RMS_EPS = 1e-5
LOG2E = 1.4426950408889634

LANES = 128
S5_CHUNK = 16
S5_SEGS = 8
SSD_Q = 128
VMEM_LIMIT = 56 * 1024 * 1024


def _dot(a, b):
    return jnp.dot(a, b, preferred_element_type=F32)


def _split3(v):
    v1 = v.astype(BF16)
    r1 = v - v1.astype(F32)
    v2 = r1.astype(BF16)
    v3 = (r1 - v2.astype(F32)).astype(BF16)
    return v1, v2, v3


def _dot_exact_lhs(m_bf16, v):
    v1, v2, v3 = _split3(v)
    return _dot(m_bf16, v1) + _dot(m_bf16, v2) + _dot(m_bf16, v3)


def _sigmoid(x):
    return 1.0 / (1.0 + jnp.exp(-x))


def _params(sem, **kw):
    return pltpu.CompilerParams(dimension_semantics=sem, vmem_limit_bytes=VMEM_LIMIT, **kw)


def _in_proj_body(x_ref, wz_ref, wxbc_ref, wdt_ref, wu_ref, wg_ref, z_ref, xbc_ref, dt_ref, u_ref, g_ref):
    x = x_ref[...]
    xh = x.astype(BF16)
    xl = (x - xh.astype(F32)).astype(BF16)
    z_ref[...] = _dot(xh, wz_ref[...]).astype(z_ref.dtype)
    xbc_ref[...] = _dot(xh, wxbc_ref[...]).astype(xbc_ref.dtype)
    u_ref[...] = _dot(xh, wu_ref[...]).astype(u_ref.dtype)
    g_ref[...] = _dot(xh, wg_ref[...]).astype(g_ref.dtype)
    d = _dot(xh, wdt_ref[...])
    nd = dt_ref.shape[-1]
    dt_ref[...] = d[:, :nd] + d[:, nd:] + _dot(xl, wdt_ref[:, :nd])


def _in_proj(x2, wz, wxbc, wdt, wu, wg, tb):
    n = x2.shape[0]
    full = lambda w: pl.BlockSpec(w.shape, lambda i: (0, 0))
    row = lambda c: pl.BlockSpec((tb, c), lambda i: (i, 0))
    nd = wdt.shape[1] // 2
    return pl.pallas_call(
        _in_proj_body,
        grid=(n // tb,),
        in_specs=[row(D_MODEL), full(wz), full(wxbc), full(wdt), full(wu), full(wg)],
        out_specs=[row(SSD_INNER), row(XBC_DIM), row(nd), row(S5_WIDTH), row(2 * D_MODEL)],
        out_shape=[
            jax.ShapeDtypeStruct((n, SSD_INNER), BF16),
            jax.ShapeDtypeStruct((n, XBC_DIM), F32),
            jax.ShapeDtypeStruct((n, nd), F32),
            jax.ShapeDtypeStruct((n, S5_WIDTH), F32),
            jax.ShapeDtypeStruct((n, 2 * D_MODEL), BF16),
        ],
        compiler_params=_params(("arbitrary",)),
        name="in_proj",
    )(x2, wz, wxbc, wdt, wu, wg)


def _conv_body(cur_ref, prev_ref, next_ref, w_ref, b_ref, xs_ref, bm_ref, cm_ref, ext_ref, *, tb, blocks_per_seq):
    i = pl.program_id(0)
    first = (i % blocks_per_seq) == 0
    last = (i % blocks_per_seq) == blocks_per_seq - 1
    ext_ref[0:8, :] = jnp.where(first, 0.0, prev_ref[0])
    ext_ref[8:8 + tb, :] = cur_ref[...]
    ext_ref[8 + tb:16 + tb, :] = jnp.where(last, 0.0, next_ref[0])
    half = SSD_CONV // 2
    cw = 512
    for c0 in range(0, XBC_DIM, cw):
        acc = jnp.broadcast_to(b_ref[:, c0:c0 + cw], (tb, cw))
        for j in range(SSD_CONV):
            acc = acc + ext_ref[8 - half + j:8 - half + j + tb, c0:c0 + cw] * w_ref[j:j + 1, c0:c0 + cw]
        y = (acc * _sigmoid(acc)).astype(BF16)
        if c0 + cw <= SSD_INNER:
            xs_ref[:, c0:c0 + cw] = y
        elif c0 < SSD_INNER + SSD_BC // 2:
            bm_ref[:, c0 - SSD_INNER:c0 - SSD_INNER + cw] = y
        else:
            o = c0 - SSD_INNER - SSD_BC // 2
            cm_ref[:, o:o + cw] = y


def _conv(xbc, conv_w, conv_b, seq, tb):
    n = xbc.shape[0]
    nb = n // tb
    xbc3 = xbc.reshape(n // 8, 8, XBC_DIM)
    r = tb // 8
    return pl.pallas_call(
        functools.partial(_conv_body, tb=tb, blocks_per_seq=seq // tb),
        grid=(nb,),
        in_specs=[
            pl.BlockSpec((tb, XBC_DIM), lambda i: (i, 0)),
            pl.BlockSpec((1, 8, XBC_DIM), lambda i: (jnp.maximum(i * r - 1, 0), 0, 0)),
            pl.BlockSpec((1, 8, XBC_DIM), lambda i: (jnp.minimum((i + 1) * r, n // 8 - 1), 0, 0)),
            pl.BlockSpec((SSD_CONV, XBC_DIM), lambda i: (0, 0)),
            pl.BlockSpec((1, XBC_DIM), lambda i: (0, 0)),
        ],
        out_specs=[
            pl.BlockSpec((tb, SSD_INNER), lambda i: (i, 0)),
            pl.BlockSpec((tb, SSD_BC // 2), lambda i: (i, 0)),
            pl.BlockSpec((tb, SSD_BC // 2), lambda i: (i, 0)),
        ],
        out_shape=[
            jax.ShapeDtypeStruct((n, SSD_INNER), BF16),
            jax.ShapeDtypeStruct((n, SSD_BC // 2), BF16),
            jax.ShapeDtypeStruct((n, SSD_BC // 2), BF16),
        ],
        scratch_shapes=[pltpu.VMEM((tb + 16, XBC_DIM), F32)],
        compiler_params=_params(("arbitrary",)),
        name="conv_silu",
    )(xbc, xbc3, xbc3, conv_w, conv_b)


def _ssd_body(xs_ref, bm_ref, cm_ref, dt_ref, par_ref, y_ref, st_ref, *, reverse, q):
    c = pl.program_id(1)

    @pl.when(c == 0)
    def _():
        st_ref[...] = jnp.zeros_like(st_ref)

    a = par_ref[0:1, :]
    bias = par_ref[1:2, :]
    dt = jax.nn.softplus(dt_ref[...] + bias)
    dta = dt * a
    ri = lax.broadcasted_iota(I32, (q, q), 0)
    ci = lax.broadcasted_iota(I32, (q, q), 1)
    mask = (ci >= ri) if reverse else (ci <= ri)
    cum = _dot_exact_lhs(mask.astype(BF16), dta)
    total = cum[0:1, :] if reverse else cum[q - 1:q, :]
    cum2 = cum * LOG2E
    tot2 = total * LOG2E
    rowp = cum2.T - jnp.log2(dt.T)
    ecum = jnp.exp2(cum2)
    wst = dt * jnp.exp2(tot2 - cum2)
    etot = jnp.exp2(tot2)
    lane = lax.broadcasted_iota(I32, (q, LANES), 1)
    lo = lane < SSD_HEADDIM
    hpg = SSD_HEADS // SSD_GROUPS
    gw = hpg * SSD_HEADDIM
    lane_g = lax.broadcasted_iota(I32, (1, gw), 1) // SSD_HEADDIM
    for g in range(SSD_GROUPS):
        bg = bm_ref[:, SSD_STATE * g:SSD_STATE * (g + 1)]
        cg = cm_ref[:, SSD_STATE * g:SSD_STATE * (g + 1)]
        cb = lax.dot_general(cg, bg, (((1,), (1,)), ((), ())), preferred_element_type=F32)
        st = st_ref[g]
        yoff = _dot(cg, st.astype(BF16))
        bgt = bg.astype(F32).T.astype(BF16)
        xw_parts = []
        for jp in range(hpg // 2):
            h0 = hpg * g + 2 * jp
            xt = xs_ref[:, SSD_HEADDIM * h0:SSD_HEADDIM * h0 + LANES]
            ws = []
            for hh in (h0, h0 + 1):
                seg = cum2[:, hh:hh + 1] - rowp[hh:hh + 1, :]
                e = jnp.exp2(jnp.where(mask, seg, -jnp.inf))
                ws.append((cb * e).astype(BF16))
            wp = jnp.concatenate(ws, axis=1)
            zero = jnp.zeros_like(xt)
            rhs = jnp.concatenate([jnp.where(lo, xt, zero), jnp.where(lo, zero, xt)], axis=0)
            yd = _dot(wp, rhs)
            ec = jnp.where(lo, ecum[:, h0:h0 + 1], ecum[:, h0 + 1:h0 + 2])
            y_ref[:, SSD_HEADDIM * h0:SSD_HEADDIM * h0 + LANES] = yd + yoff[:, LANES * jp:LANES * (jp + 1)] * ec
            wc = jnp.where(lo, wst[:, h0:h0 + 1], wst[:, h0 + 1:h0 + 2])
            xw_parts.append((xt.astype(F32) * wc).astype(BF16))
        xw = jnp.concatenate(xw_parts, axis=1)
        snew = _dot(bgt, xw)
        eg = jnp.zeros((1, gw), F32)
        for j in range(hpg):
            eg = jnp.where(lane_g == j, etot[:, hpg * g + j:hpg * g + j + 1], eg)
        st_ref[g] = st * eg + snew


def _ssd(xs, bm, cm, dt, par, batch, seq, reverse):
    n = xs.shape[0]
    q = SSD_Q
    nc = seq // q
    d = 1 if reverse else 0

    def blk(b, c):
        return b * nc + (nc - 1 - c if reverse else c)

    gw = (SSD_HEADS // SSD_GROUPS) * SSD_HEADDIM
    return pl.pallas_call(
        functools.partial(_ssd_body, reverse=reverse, q=q),
        grid=(batch, nc),
        in_specs=[
            pl.BlockSpec((q, SSD_INNER), lambda b, c: (blk(b, c), 0)),
            pl.BlockSpec((q, SSD_BC // 2), lambda b, c: (blk(b, c), 0)),
            pl.BlockSpec((q, SSD_BC // 2), lambda b, c: (blk(b, c), 0)),
            pl.BlockSpec((q, LANES), lambda b, c: (blk(b, c), d)),
            pl.BlockSpec((8, LANES), lambda b, c: (0, 0)),
        ],
        out_specs=pl.BlockSpec((q, SSD_INNER), lambda b, c: (blk(b, c), 0)),
        out_shape=jax.ShapeDtypeStruct((n, SSD_INNER), F32),
        scratch_shapes=[pltpu.VMEM((SSD_GROUPS, SSD_STATE, gw), F32)],
        compiler_params=_params(("arbitrary", "arbitrary")),
        name="ssd_bwd" if reverse else "ssd_fwd",
    )(xs, bm, cm, dt, par)


def _s5_mats(a_re, a_im, log_step, b_re, b_im, c_re, c_im, reverse):
    t = S5_CHUNK
    hp = lax.Precision.HIGHEST
    step = jnp.exp(log_step.astype(F32))[:, None]
    lr = a_re.astype(F32) * step
    li = a_im.astype(F32) * step
    pw = jnp.arange(t + 1, dtype=F32)[None, :, None]
    mag = jnp.exp(lr[:, None, :] * pw)
    pr = mag * jnp.cos(li[:, None, :] * pw)
    pi = mag * jnp.sin(li[:, None, :] * pw)
    ar, ai = pr[:, 1], pi[:, 1]
    den = a_re.astype(F32) ** 2 + a_im.astype(F32) ** 2
    qr = ((ar - 1.0) * a_re + ai * a_im) / den
    qi = (ai * a_re - (ar - 1.0) * a_im) / den
    br = qr[..., None] * b_re - qi[..., None] * b_im
    bi = qr[..., None] * b_im + qi[..., None] * b_re
    cpr = c_re[:, None] * pr[:, :t, None, :] - c_im[:, None] * pi[:, :t, None, :]
    cpi = c_re[:, None] * pi[:, :t, None, :] + c_im[:, None] * pr[:, :t, None, :]
    m = jnp.einsum("gtkp,gpj->gtkj", cpr, br, precision=hp) - jnp.einsum("gtkp,gpj->gtkj", cpi, bi, precision=hp)
    s_i = jnp.arange(t)[:, None]
    t_i = jnp.arange(t)[None, :]
    tau = (s_i - t_i) if reverse else (t_i - s_i)
    mt = m[:, jnp.clip(tau, 0, t - 1)]
    mt = jnp.where((tau >= 0)[None, :, :, None, None], mt, 0.0)
    toep = mt.transpose(0, 1, 4, 2, 3).reshape(S5_GROUPS, t * S5_GROUP, t * S5_GROUP)
    e_in = (jnp.arange(t) if reverse else (t - 1 - jnp.arange(t)))
    wr = pr[:, e_in][..., None] * br[:, None] - pi[:, e_in][..., None] * bi[:, None]
    wi = pr[:, e_in][..., None] * bi[:, None] + pi[:, e_in][..., None] * br[:, None]
    win_re = wr.transpose(0, 1, 3, 2).reshape(S5_GROUPS, t * S5_GROUP, S5_STATE)
    win_im = wi.transpose(0, 1, 3, 2).reshape(S5_GROUPS, t * S5_GROUP, S5_STATE)
    e_out = (t - jnp.arange(t)) if reverse else (jnp.arange(t) + 1)
    gr = c_re[:, None] * pr[:, e_out, None, :] - c_im[:, None] * pi[:, e_out, None, :]
    gi = c_re[:, None] * pi[:, e_out, None, :] + c_im[:, None] * pr[:, e_out, None, :]
    wout_re = gr.transpose(0, 3, 1, 2).reshape(S5_GROUPS, S5_STATE, t * S5_GROUP)
    wout_im = (-gi).transpose(0, 3, 1, 2).reshape(S5_GROUPS, S5_STATE, t * S5_GROUP)
    return toep, win_re, win_im, wout_re, wout_im, (lr, li)


def _pair_in(w):
    g, r, p = w.shape
    w4 = w.reshape(g // 2, 2, r, p)
    eye = jnp.eye(2, dtype=w.dtype)
    return (w4[:, :, :, None, :] * eye[None, :, None, :, None]).reshape(g // 2, 2 * r, 2 * p)


def _cmul(xr, xi, ar, ai):
    return xr * ar - xi * ai, xr * ai + xi * ar


def _s5_body(u_ref, toep_ref, win_ref, wout_ref, co_ref, y_ref, s_ref, x_ref, *, nseg_rows):
    u0 = u_ref[0]
    u1 = u_ref[1]
    up = jnp.concatenate([u0, u1], axis=1)
    for k in range(4):
        s_ref[k] = _dot(up, win_ref[0, k])
    co = co_ref[0]
    nr = nseg_rows
    zero = jnp.zeros((S5_SEGS, LANES), F32)

    def scan(kr, ki, ar, ai, asr, asi, reverse):
        def rows(i):
            r = (nr - 1 - i) if reverse else i
            return pl.ds(pl.multiple_of(r * S5_SEGS, S5_SEGS), S5_SEGS)

        def p1(i, c):
            xr, xi = _cmul(c[0], c[1], ar, ai)
            return xr + s_ref[kr, rows(i), :], xi + s_ref[ki, rows(i), :]

        er, ei = lax.fori_loop(0, nr, p1, (zero, zero))
        order = range(S5_SEGS - 1, -1, -1) if reverse else range(S5_SEGS)
        cr = jnp.zeros((1, LANES), F32)
        ci = jnp.zeros((1, LANES), F32)
        crs, cis = [None] * S5_SEGS, [None] * S5_SEGS
        for j in order:
            crs[j], cis[j] = cr, ci
            nr_, ni_ = _cmul(cr, ci, asr, asi)
            cr, ci = nr_ + er[j:j + 1], ni_ + ei[j:j + 1]
        c0 = (jnp.concatenate(crs, axis=0), jnp.concatenate(cis, axis=0))

        def p2(i, c):
            x_ref[kr, rows(i), :] = c[0]
            x_ref[ki, rows(i), :] = c[1]
            xr, xi = _cmul(c[0], c[1], ar, ai)
            return xr + s_ref[kr, rows(i), :], xi + s_ref[ki, rows(i), :]

        lax.fori_loop(0, nr, p2, c0)

    scan(0, 1, co[0:1], co[1:2], co[2:3], co[3:4], False)
    scan(2, 3, co[4:5], co[5:6], co[6:7], co[7:8], True)
    yc = _dot(x_ref[0].astype(BF16), wout_ref[0, 0])
    for k in range(1, 4):
        yc = yc + _dot(x_ref[k].astype(BF16), wout_ref[0, k])
    w = u0.shape[1]
    y_ref[0] = _dot(u0, toep_ref[0, 0]) + yc[:, :w]
    y_ref[1] = _dot(u1, toep_ref[0, 1]) + yc[:, w:]


def _s5(u_l, toep, win, wout, coef, batch, nc):
    g, _, w = u_l.shape
    npair = g // 2
    return pl.pallas_call(
        functools.partial(_s5_body, nseg_rows=nc // S5_SEGS),
        grid=(batch, npair),
        in_specs=[
            pl.BlockSpec((2, nc, w), lambda b, p: (p, b, 0)),
            pl.BlockSpec((1, 2, w, w), lambda b, p: (p, 0, 0, 0)),
            pl.BlockSpec((1, 4, 2 * w, LANES), lambda b, p: (p, 0, 0, 0)),
            pl.BlockSpec((1, 4, LANES, 2 * w), lambda b, p: (p, 0, 0, 0)),
            pl.BlockSpec((1, 8, LANES), lambda b, p: (p, 0, 0)),
        ],
        out_specs=pl.BlockSpec((2, nc, w), lambda b, p: (p, b, 0)),
        out_shape=jax.ShapeDtypeStruct(u_l.shape, F32),
        scratch_shapes=[pltpu.VMEM((4, nc, LANES), F32), pltpu.VMEM((4, nc, LANES), F32)],
        compiler_params=_params(("arbitrary", "arbitrary")),
        name="s5_scan",
    )(u_l, toep, win, wout, coef)


def _layer_norm(r, g, b):
    mu = jnp.mean(r, axis=-1, keepdims=True)
    d = r - mu
    var = jnp.mean(d * d, axis=-1, keepdims=True)
    return d * lax.rsqrt(var + LN_EPS) * g + b


def _mid_body(yf_ref, yb_ref, xs_ref, z_ref, s5_ref, u_ref, g_ref, x_ref, dexp_ref, ng_ref, wout_ref, s5d_ref,
              wga_ref, wgb_ref, wo_ref, l1g_ref, l1b_ref, wr_ref, h_ref, lg_ref):
    y = yf_ref[...] + yb_ref[...] + xs_ref[...].astype(F32) * dexp_ref[...]
    z = z_ref[...].astype(F32)
    y = y * (z * _sigmoid(z))
    gw = SSD_INNER // SSD_GROUPS
    parts = []
    for g in range(SSD_GROUPS):
        yg = y[:, gw * g:gw * (g + 1)]
        ms = jnp.mean(yg * yg, axis=-1, keepdims=True)
        parts.append(yg * lax.rsqrt(ms + RMS_EPS))
    yn = (jnp.concatenate(parts, axis=1) * ng_ref[...]).astype(BF16)
    ya = _dot(yn, wout_ref[...])
    v = s5_ref[...] + u_ref[...] * s5d_ref[...]
    v = jax.nn.gelu(v).astype(BF16)
    yb = _dot(v, wga_ref[...]) * _sigmoid(_dot(v, wgb_ref[...]))
    gt = g_ref[...].astype(F32)
    mix = _sigmoid(gt[:, :D_MODEL]) * ya + _sigmoid(gt[:, D_MODEL:]) * yb
    r = DEEPNORM_ALPHA * x_ref[...] + _dot(mix.astype(BF16), wo_ref[...])
    h = _layer_norm(r, l1g_ref[...], l1b_ref[...])
    h_ref[...] = h
    hh = h.astype(BF16)
    hl = (h - hh.astype(F32)).astype(BF16)
    d = _dot(hh, wr_ref[...])
    lg_ref[...] = d[:, :LANES] + d[:, LANES:] + _dot(hl, wr_ref[:, :LANES])


def _mid(yf, yb, xs, z, s5y, u, gates, x2, dexp, ng, wout, s5d, wga, wgb, wo, l1g, l1b, wr, tb):
    n = x2.shape[0]
    row = lambda a: pl.BlockSpec((tb, a.shape[1]), lambda i: (i, 0))
    full = lambda a: pl.BlockSpec(a.shape, lambda i: (0, 0))
    acts = [yf, yb, xs, z, s5y, u, gates, x2]
    wts = [dexp, ng, wout, s5d, wga, wgb, wo, l1g, l1b, wr]
    return pl.pallas_call(
        _mid_body,
        grid=(n // tb,),
        in_specs=[row(a) for a in acts] + [full(w) for w in wts],
        out_specs=[pl.BlockSpec((tb, D_MODEL), lambda i: (i, 0)), pl.BlockSpec((tb, LANES), lambda i: (i, 0))],
        out_shape=[jax.ShapeDtypeStruct((n, D_MODEL), F32), jax.ShapeDtypeStruct((n, LANES), F32)],
        compiler_params=_params(("arbitrary",)),
        name="merge_ln1",
    )(*acts, *wts)


def _topk_body(lg_ref, gsel_ref, pos_ref, base_ref, aff_ref, *, seq, cap, tb):
    lane = lax.broadcasted_iota(I32, (seq, LANES), 1)
    valid = lane < N_EXPERTS
    lg = jnp.where(valid, lg_ref[...], -jnp.inf)
    m = jnp.max(lg, axis=1, keepdims=True)
    e = jnp.exp(lg - m)
    aff_ref[...] = e / jnp.sum(e, axis=1, keepdims=True)

    def count_ge(t_bits):
        t = pltpu.bitcast(t_bits, F32)
        return jnp.sum((aff_ref[...] >= t).astype(I32), axis=0, keepdims=True)

    def bs(_, c):
        lo, hi = c
        mid = lo + ((hi - lo) >> 1)
        ge = count_ge(mid) >= cap
        return jnp.where(ge, mid, lo), jnp.where(ge, hi, mid)

    lo0 = jnp.zeros((1, LANES), I32)
    hi0 = jnp.full((1, LANES), 0x3F800001, I32)
    thr_bits, _ = lax.fori_loop(0, 31, bs, (lo0, hi0))
    thr = pltpu.bitcast(thr_bits, F32)
    nxt = pltpu.bitcast(thr_bits + 1, F32)
    n_gt = count_ge(thr_bits + 1)
    need = (cap - n_gt).astype(F32)
    ri = lax.broadcasted_iota(I32, (tb, tb), 0)
    ci = lax.broadcasted_iota(I32, (tb, tb), 1)
    tri = (ci < ri).astype(BF16)
    vrow = lax.broadcasted_iota(I32, (1, LANES), 1) < N_EXPERTS

    def blk(j, c):
        ceq, csel = c
        rows = pl.ds(pl.multiple_of(j * tb, tb), tb)
        a = aff_ref[rows, :]
        gt = a >= nxt
        eq = (a >= thr) & jnp.logical_not(gt)
        eqc = ceq + _dot(tri, eq.astype(BF16))
        sel = (gt | (eq & (eqc < need))) & vrow
        self = sel.astype(F32)
        pos = csel + _dot(tri, self.astype(BF16))
        pos_ref[rows, :] = jnp.where(sel, pos, -1.0).astype(I32)
        gsel_ref[rows, :] = jnp.where(sel, a, 0.0)
        base_ref[0, pl.ds(j, 1), :] = csel.astype(I32)
        return (ceq + jnp.sum(eq.astype(F32), axis=0, keepdims=True),
                csel + jnp.sum(self, axis=0, keepdims=True))

    z = jnp.zeros((1, LANES), F32)
    lax.fori_loop(0, seq // tb, blk, (z, z))


def _topk(lg, batch, seq, cap, tb):
    nb = seq // tb
    return pl.pallas_call(
        functools.partial(_topk_body, seq=seq, cap=cap, tb=tb),
        grid=(batch,),
        in_specs=[pl.BlockSpec((seq, LANES), lambda b: (b, 0))],
        out_specs=[
            pl.BlockSpec((seq, LANES), lambda b: (b, 0)),
            pl.BlockSpec((seq, LANES), lambda b: (b, 0)),
            pl.BlockSpec((1, nb, LANES), lambda b: (b, 0, 0)),
        ],
        out_shape=[
            jax.ShapeDtypeStruct((batch * seq, LANES), F32),
            jax.ShapeDtypeStruct((batch * seq, LANES), I32),
            jax.ShapeDtypeStruct((batch, nb, LANES), I32),
        ],
        scratch_shapes=[pltpu.VMEM((seq, LANES), F32)],
        compiler_params=_params(("arbitrary",)),
        name="topk_select",
    )(lg)


def _compact_body(base_ref, pos_ref, idx_ref, *, nb, tb):
    b = pl.program_id(0)
    idx_ref[...] = jnp.zeros_like(idx_ref)
    lane = lax.broadcasted_iota(I32, (tb, LANES), 1)
    trow = lax.broadcasted_iota(I32, (tb, LANES), 0)

    def blk(j, carry):
        rows = pl.ds(pl.multiple_of(j * tb, tb), tb)
        p = pos_ref[rows, :]
        tok = trow + j * tb
        for e in range(N_EXPERTS):
            wb = base_ref[(b * nb + j) * N_EXPERTS + e] // LANES
            rel = p[:, e:e + 1] - wb * LANES
            lo = jnp.sum(jnp.where(rel == lane, tok, 0), axis=0, keepdims=True)
            hi = jnp.sum(jnp.where(rel - LANES == lane, tok, 0), axis=0, keepdims=True)
            idx_ref[0, e, pl.ds(wb, 1), :] += lo
            idx_ref[0, e, pl.ds(wb + 1, 1), :] += hi
        return carry

    lax.fori_loop(0, nb, blk, 0)


def _compact(base_flat, pos, batch, seq, cap, tb):
    nb = seq // tb
    rows = cap // LANES + 8
    return pl.pallas_call(
        functools.partial(_compact_body, nb=nb, tb=tb),
        grid_spec=pltpu.PrefetchScalarGridSpec(
            num_scalar_prefetch=1,
            grid=(batch,),
            in_specs=[pl.BlockSpec((seq, LANES), lambda b, base: (b, 0))],
            out_specs=pl.BlockSpec((1, N_EXPERTS, rows, LANES), lambda b, base: (b, 0, 0, 0)),
        ),
        out_shape=jax.ShapeDtypeStruct((batch, N_EXPERTS, rows, LANES), I32),
        compiler_params=_params(("arbitrary",)),
        name="slot_compact",
    )(base_flat, pos)


def _ffn_body(idx_ref, idxn_ref, h3_ref, w1_ref, w3_ref, w2_ref, y_ref, xbuf_ref, xb_ref, acc_ref, sem,
              *, nf, nbatch, seq, cap):
    e = pl.program_id(0)
    b = pl.program_id(1)
    f = pl.program_id(2)
    eb = e * nbatch + b
    neb = pl.num_programs(0) * nbatch
    slot = eb % 2
    nk = D_MODEL // LANES

    def issue(iref, tok0, sl):
        def body(s, c):
            t = iref[0, 0, s] + tok0
            pltpu.make_async_copy(h3_ref.at[t], xbuf_ref.at[sl, :, s, :], sem.at[sl]).start()
            return c

        lax.fori_loop(0, cap, body, 0, unroll=8)

    @pl.when((f == 0) & (eb == 0))
    def _():
        issue(idx_ref, b * seq, slot)

    @pl.when(f == 0)
    def _():
        pltpu.make_async_copy(xbuf_ref.at[slot], xbuf_ref.at[slot], sem.at[slot]).wait()
        for k in range(nk):
            xb_ref[:, LANES * k:LANES * (k + 1)] = xbuf_ref[slot, k].astype(BF16)

    @pl.when((f == 1) & (eb + 1 < neb))
    def _():
        issue(idxn_ref, ((eb + 1) % nbatch) * seq, 1 - slot)

    x = xb_ref[...]
    a = _dot(x, w1_ref[0].astype(BF16))
    b = _dot(x, w3_ref[0].astype(BF16))
    hid = (a * _sigmoid(a) * b).astype(BF16)
    p = _dot(hid, w2_ref[0].astype(BF16))

    @pl.when(f == 0)
    def _():
        acc_ref[...] = p

    @pl.when(f > 0)
    def _():
        acc_ref[...] += p

    @pl.when(f == nf - 1)
    def _():
        y_ref[0, 0] = acc_ref[...].astype(y_ref.dtype)


def _ffn(idx3, h3, w1, w3, w2, batch, seq, cap, ft):
    ne, d, ff = w1.shape
    nf = ff // ft
    assert nf >= 2
    neb = ne * batch

    def nxt(e, b, f):
        eb1 = jnp.minimum(e * batch + b + 1, neb - 1)
        return ((eb1 % batch) * ne + eb1 // batch, 0, 0)

    return pl.pallas_call(
        functools.partial(_ffn_body, nf=nf, nbatch=batch, seq=seq, cap=cap),
        grid=(ne, batch, nf),
        in_specs=[
            pl.BlockSpec((1, 1, cap), lambda e, b, f: (b * ne + e, 0, 0), memory_space=pltpu.SMEM),
            pl.BlockSpec((1, 1, cap), nxt, memory_space=pltpu.SMEM),
            pl.BlockSpec(memory_space=pl.ANY),
            pl.BlockSpec((1, d, ft), lambda e, b, f: (e, 0, f)),
            pl.BlockSpec((1, d, ft), lambda e, b, f: (e, 0, f)),
            pl.BlockSpec((1, ft, d), lambda e, b, f: (e, f, 0)),
        ],
        out_specs=pl.BlockSpec((1, 1, cap, d), lambda e, b, f: (b, e, 0, 0)),
        out_shape=jax.ShapeDtypeStruct((batch, ne, cap, d), BF16),
        scratch_shapes=[
            pltpu.VMEM((2, d // LANES, cap, LANES), F32),
            pltpu.VMEM((cap, d), BF16),
            pltpu.VMEM((cap, d), F32),
            pltpu.SemaphoreType.DMA((2,)),
        ],
        compiler_params=_params(("arbitrary", "arbitrary", "arbitrary")),
        name="expert_ffn",
    )(idx3, idx3, h3, w1, w3, w2)


def _combine_body(base_ref, pos_ref, gsel_ref, h_ref, y_ref, g_ref, b_ref, o_ref, win_ref, sem, *, nb, tb, cap, win):
    bi = pl.program_id(0)
    j = pl.program_id(1)
    step = bi * nb + j
    nsteps = pl.num_programs(0) * nb
    slot = step % 2

    def w0_of(s, e):
        base = base_ref[s * N_EXPERTS + e]
        return pl.multiple_of(jnp.minimum((base // 16) * 16, cap - win), 16)

    def copies(s, sl):
        b_s = s // nb
        return [pltpu.make_async_copy(y_ref.at[b_s, e, pl.ds(w0_of(s, e), win), :], win_ref.at[sl, e], sem.at[sl])
                for e in range(N_EXPERTS)]

    @pl.when(step == 0)
    def _():
        for cp in copies(step, slot):
            cp.start()

    @pl.when(step + 1 < nsteps)
    def _():
        for cp in copies(step + 1, 1 - slot):
            cp.start()

    for cp in copies(step, slot):
        cp.wait()

    p = pos_ref[...]
    gs = gsel_ref[...]
    lane = lax.broadcasted_iota(I32, (tb, win), 1)
    acc = jnp.zeros((tb, D_MODEL), F32)
    for e in range(N_EXPERTS):
        rel = p[:, e:e + 1] - w0_of(step, e)
        s_e = jnp.where(rel == lane, gs[:, e:e + 1], 0.0).astype(BF16)
        acc = acc + _dot(s_e, win_ref[slot, e])
    r = DEEPNORM_ALPHA * h_ref[...] + acc
    o_ref[...] = _layer_norm(r, g_ref[...], b_ref[...])


def _combine(base_flat, pos, gsel, h, y, g, b, batch, seq, cap, tb):
    nb = seq // tb
    win = tb + 16
    return pl.pallas_call(
        functools.partial(_combine_body, nb=nb, tb=tb, cap=cap, win=win),
        grid_spec=pltpu.PrefetchScalarGridSpec(
            num_scalar_prefetch=1,
            grid=(batch, nb),
            in_specs=[
                pl.BlockSpec((tb, LANES), lambda bi, j, base: (bi * nb + j, 0)),
                pl.BlockSpec((tb, LANES), lambda bi, j, base: (bi * nb + j, 0)),
                pl.BlockSpec((tb, D_MODEL), lambda bi, j, base: (bi * nb + j, 0)),
                pl.BlockSpec(memory_space=pl.ANY),
                pl.BlockSpec((1, D_MODEL), lambda bi, j, base: (0, 0)),
                pl.BlockSpec((1, D_MODEL), lambda bi, j, base: (0, 0)),
            ],
            out_specs=pl.BlockSpec((tb, D_MODEL), lambda bi, j, base: (bi * nb + j, 0)),
            scratch_shapes=[pltpu.VMEM((2, N_EXPERTS, win, D_MODEL), BF16), pltpu.SemaphoreType.DMA((2,))],
        ),
        out_shape=jax.ShapeDtypeStruct((batch * seq, D_MODEL), F32),
        compiler_params=_params(("arbitrary", "arbitrary")),
        name="combine_ln2",
    )(base_flat, pos, gsel, h, y, g, b)


def _pad_lanes(v, width=LANES):
    return jnp.pad(v, [(0, 0)] * (v.ndim - 1) + [(0, width - v.shape[-1])])


def _hi_lo(w):
    hi = w.astype(BF16)
    lo = (w - hi.astype(F32)).astype(BF16)
    return jnp.concatenate([hi, lo], axis=1)


def _layer(x, w_in, conv_w, conv_b, ssd_a_log_f, ssd_a_log_b, ssd_dt_bias_f, ssd_dt_bias_b, ssd_d, ssd_norm_g,
           w_out_ssd, s5_a_re_f, s5_a_im_f, s5_log_step_f, s5_a_re_b, s5_a_im_b, s5_log_step_b, s5_b_re, s5_b_im,
           s5_c_re, s5_c_im, s5_d, w_glu_a, w_glu_b, w_o, ln1_g, ln1_b, w_router, w_e1, w_e3, w_e2, ln2_g, ln2_b):
    batch, seq, d = x.shape
    n = batch * seq
    x2 = x.reshape(n, d)
    c0, c1, c2, c3 = SSD_INNER, SSD_INNER + XBC_DIM, SSD_INNER + XBC_DIM + 2 * SSD_HEADS, \
        SSD_INNER + XBC_DIM + 2 * SSD_HEADS + S5_WIDTH
    wdt = w_in[:, c1:c2]
    wdt = jnp.concatenate([_pad_lanes(wdt[:, :SSD_HEADS]), _pad_lanes(wdt[:, SSD_HEADS:])], axis=1)
    z, xbc, dt, u, gates = _in_proj(x2, w_in[:, :c0].astype(BF16), w_in[:, c0:c1].astype(BF16), _hi_lo(wdt),
                                    w_in[:, c2:c3].astype(BF16), w_in[:, c3:].astype(BF16), tb=256)

    t = S5_CHUNK
    nc = seq // t
    nr = nc // S5_SEGS
    u_l = u.astype(BF16).reshape(batch, S5_SEGS, nr, t, S5_GROUPS, S5_GROUP)
    u_l = u_l.transpose(4, 0, 2, 1, 3, 5).reshape(S5_GROUPS, batch * nc, t * S5_GROUP)
    mf = _s5_mats(s5_a_re_f, s5_a_im_f, s5_log_step_f, s5_b_re, s5_b_im, s5_c_re, s5_c_im, False)
    mb = _s5_mats(s5_a_re_b, s5_a_im_b, s5_log_step_b, s5_b_re, s5_b_im, s5_c_re, s5_c_im, True)
    npair = S5_GROUPS // 2
    toep = (mf[0] + mb[0]).astype(BF16).reshape(npair, 2, t * S5_GROUP, t * S5_GROUP)
    win = jnp.stack([_pair_in(mf[1]), _pair_in(mf[2]), _pair_in(mb[1]), _pair_in(mb[2])], axis=1).astype(BF16)
    wout = jnp.stack([_pair_in(m.transpose(0, 2, 1)).transpose(0, 2, 1) for m in (mf[3], mf[4], mb[3], mb[4])],
                     axis=1).astype(BF16)

    def powc(lr, li, k):
        mag = jnp.exp(lr * k)
        return (mag * jnp.cos(li * k)).reshape(npair, LANES), (mag * jnp.sin(li * k)).reshape(npair, LANES)

    coef = jnp.stack([*powc(*mf[5], float(t)), *powc(*mf[5], float(t * nr)),
                      *powc(*mb[5], float(t)), *powc(*mb[5], float(t * nr))], axis=1)
    s5y = _s5(u_l, toep, win, wout, coef, batch, nc)
    s5y = s5y.reshape(S5_GROUPS, batch, nr, S5_SEGS, t, S5_GROUP).transpose(1, 3, 2, 4, 0, 5).reshape(n, S5_WIDTH)

    xs, bm, cm = _conv(xbc, conv_w, conv_b.reshape(1, XBC_DIM), seq, tb=512)

    def ssd_par(a_log, bias):
        p = jnp.zeros((8, LANES), F32)
        return p.at[0, :SSD_HEADS].set(-jnp.exp(a_log)).at[1, :SSD_HEADS].set(bias)

    yb = _ssd(xs, bm, cm, dt, ssd_par(ssd_a_log_b, ssd_dt_bias_b), batch, seq, True)
    yf = _ssd(xs, bm, cm, dt, ssd_par(ssd_a_log_f, ssd_dt_bias_f), batch, seq, False)

    wr = _hi_lo(_pad_lanes(w_router))
    h, lg = _mid(yf, yb, xs, z, s5y, u, gates, x2, jnp.repeat(ssd_d, SSD_HEADDIM).reshape(1, SSD_INNER),
                 ssd_norm_g.reshape(1, SSD_INNER), w_out_ssd.astype(BF16), s5_d.reshape(1, S5_WIDTH),
                 w_glu_a.astype(BF16), w_glu_b.astype(BF16), w_o.astype(BF16), ln1_g.reshape(1, d),
                 ln1_b.reshape(1, d), wr, tb=256)

    cap = CAPACITY_FACTOR * seq // N_EXPERTS
    tbk = 128
    gsel, pos, base = _topk(lg, batch, seq, cap, tbk)
    base_flat = base[:, :, :N_EXPERTS].reshape(-1)
    idx = _compact(base_flat, pos, batch, seq, cap, tbk)
    idx3 = idx[:, :, :cap // LANES, :].reshape(batch * N_EXPERTS, 1, cap)
    y = _ffn(idx3, h.reshape(n, d // LANES, LANES), w_e1, w_e3, w_e2, batch, seq, cap, ft=256)
    out = _combine(base_flat, pos, gsel, h, y, ln2_g.reshape(1, d), ln2_b.reshape(1, d), batch, seq, cap, tbk)
    return out.reshape(batch, seq, d)


def kernel(x, w_in, conv_w, conv_b, ssd_a_log_f, ssd_a_log_b, ssd_dt_bias_f, ssd_dt_bias_b, ssd_d, ssd_norm_g, w_out_ssd, s5_a_re_f, s5_a_im_f, s5_log_step_f, s5_a_re_b, s5_a_im_b, s5_log_step_b, s5_b_re, s5_b_im, s5_c_re, s5_c_im, s5_d, w_glu_a, w_glu_b, w_o, ln1_g, ln1_b, w_router, w_e1, w_e3, w_e2, ln2_g, ln2_b):
    h = x
    for i in range(DEPTH):
        h = _layer(h, w_in[i], conv_w[i], conv_b[i], ssd_a_log_f[i], ssd_a_log_b[i], ssd_dt_bias_f[i],
                   ssd_dt_bias_b[i], ssd_d[i], ssd_norm_g[i], w_out_ssd[i], s5_a_re_f[i], s5_a_im_f[i],
                   s5_log_step_f[i], s5_a_re_b[i], s5_a_im_b[i], s5_log_step_b[i], s5_b_re[i], s5_b_im[i],
                   s5_c_re[i], s5_c_im[i], s5_d[i], w_glu_a[i], w_glu_b[i], w_o[i], ln1_g[i], ln1_b[i],
                   w_router[i], w_e1[i], w_e3[i], w_e2[i], ln2_g[i], ln2_b[i])
    return h
```

```python
import functools
import math

import jax
import jax.numpy as jnp
import numpy as np
from jax import lax
from jax.experimental import pallas as pl
from jax.experimental.pallas import tpu as pltpu

F32 = jnp.float32
BF16 = jnp.bfloat16
I32 = jnp.int32

D_MODEL = 1024
SSD_HEADDIM = 64
SSD_HEADS = 24
SSD_INNER = SSD_HEADS * SSD_HEADDIM
SSD_GROUPS = 4
SSD_STATE = 128
SSD_CONV = 5
SSD_BC = 2 * SSD_GROUPS * SSD_STATE
XBC_DIM = SSD_INNER + SSD_BC
S5_GROUP = 16
S5_WIDTH = 768
S5_GROUPS = S5_WIDTH // S5_GROUP
S5_STATE = 64
N_EXPERTS = 16
EXPERT_FF = 2816
CAPACITY_FACTOR = 2
DEPTH = 1
DEEPNORM_ALPHA = (2.0 * DEPTH) ** 0.25
LN_EPS = 1e-5
RMS_EPS = 1e-5
LOG2E = 1.4426950408889634

LANES = 128
S5_CHUNK = 16
S5_SEGS = 8
SSD_Q = 128
CONV_HALO = 16
VMEM_LIMIT = 56 * 1024 * 1024


def _dot(a, b):
    return jnp.dot(a, b, preferred_element_type=F32)


def _split3(v):
    v1 = v.astype(BF16)
    r1 = v - v1.astype(F32)
    v2 = r1.astype(BF16)
    v3 = (r1 - v2.astype(F32)).astype(BF16)
    return v1, v2, v3


def _dot_exact_lhs(m_bf16, v):
    v1, v2, v3 = _split3(v)
    return _dot(m_bf16, v1) + _dot(m_bf16, v2) + _dot(m_bf16, v3)


def _sigmoid(x):
    return 1.0 / (1.0 + jnp.exp(-x))


def _params(sem, **kw):
    return pltpu.CompilerParams(dimension_semantics=sem, vmem_limit_bytes=VMEM_LIMIT, **kw)


def _in_proj_body(x_ref, wz_ref, wxbc_ref, wdt_ref, wu_ref, wg_ref, z_ref, xbc_ref, dt_ref, u_ref, g_ref):
    x = x_ref[...]
    xh = x.astype(BF16)
    xl = (x - xh.astype(F32)).astype(BF16)
    z_ref[...] = _dot(xh, wz_ref[...]).astype(z_ref.dtype)
    xbc_ref[...] = _dot(xh, wxbc_ref[...]).astype(xbc_ref.dtype)
    u_ref[...] = _dot(xh, wu_ref[...]).astype(u_ref.dtype)
    g_ref[...] = _dot(xh, wg_ref[...]).astype(g_ref.dtype)
    d = _dot(xh, wdt_ref[...])
    nd = dt_ref.shape[-1]
    dt_ref[...] = d[:, :nd] + d[:, nd:] + _dot(xl, wdt_ref[:, :nd])


def _in_proj(x2, wz, wxbc, wdt, wu, wg, tb):
    n = x2.shape[0]
    full = lambda w: pl.BlockSpec(w.shape, lambda i: (0, 0))
    row = lambda c: pl.BlockSpec((tb, c), lambda i: (i, 0))
    nd = wdt.shape[1] // 2
    return pl.pallas_call(
        _in_proj_body,
        grid=(n // tb,),
        in_specs=[row(D_MODEL), full(wz), full(wxbc), full(wdt), full(wu), full(wg)],
        out_specs=[row(SSD_INNER), row(XBC_DIM), row(nd), row(S5_WIDTH), row(2 * D_MODEL)],
        out_shape=[
            jax.ShapeDtypeStruct((n, SSD_INNER), BF16),
            jax.ShapeDtypeStruct((n, XBC_DIM), BF16),
            jax.ShapeDtypeStruct((n, nd), F32),
            jax.ShapeDtypeStruct((n, S5_WIDTH), BF16),
            jax.ShapeDtypeStruct((n, 2 * D_MODEL), BF16),
        ],
        compiler_params=_params(("arbitrary",)),
        name="in_proj",
    )(x2, wz, wxbc, wdt, wu, wg)


def _conv_body(cur_ref, prev_ref, next_ref, w_ref, b_ref, xs_ref, bm_ref, cm_ref, ext_ref, *, tb, blocks_per_seq):
    i = pl.program_id(0)
    first = (i % blocks_per_seq) == 0
    last = (i % blocks_per_seq) == blocks_per_seq - 1
    hr = CONV_HALO
    ext_ref[0:hr, :] = jnp.where(first, 0.0, prev_ref[0].astype(F32))
    ext_ref[hr:hr + tb, :] = cur_ref[...].astype(F32)
    ext_ref[hr + tb:2 * hr + tb, :] = jnp.where(last, 0.0, next_ref[0].astype(F32))
    half = SSD_CONV // 2
    cw = 512
    for c0 in range(0, XBC_DIM, cw):
        acc = jnp.broadcast_to(b_ref[:, c0:c0 + cw], (tb, cw))
        for j in range(SSD_CONV):
            acc = acc + ext_ref[hr - half + j:hr - half + j + tb, c0:c0 + cw] * w_ref[j:j + 1, c0:c0 + cw]
        y = (acc * _sigmoid(acc)).astype(BF16)
        if c0 + cw <= SSD_INNER:
            xs_ref[:, c0:c0 + cw] = y
        elif c0 < SSD_INNER + SSD_BC // 2:
            bm_ref[:, c0 - SSD_INNER:c0 - SSD_INNER + cw] = y
        else:
            o = c0 - SSD_INNER - SSD_BC // 2
            cm_ref[:, o:o + cw] = y


def _conv(xbc, conv_w, conv_b, seq, tb):
    n = xbc.shape[0]
    nb = n // tb
    hr = CONV_HALO
    xbc3 = xbc.reshape(n // hr, hr, XBC_DIM)
    r = tb // hr
    return pl.pallas_call(
        functools.partial(_conv_body, tb=tb, blocks_per_seq=seq // tb),
        grid=(nb,),
        in_specs=[
            pl.BlockSpec((tb, XBC_DIM), lambda i: (i, 0)),
            pl.BlockSpec((1, hr, XBC_DIM), lambda i: (jnp.maximum(i * r - 1, 0), 0, 0)),
            pl.BlockSpec((1, hr, XBC_DIM), lambda i: (jnp.minimum((i + 1) * r, n // hr - 1), 0, 0)),
            pl.BlockSpec((SSD_CONV, XBC_DIM), lambda i: (0, 0)),
            pl.BlockSpec((1, XBC_DIM), lambda i: (0, 0)),
        ],
        out_specs=[
            pl.BlockSpec((tb, SSD_INNER), lambda i: (i, 0)),
            pl.BlockSpec((tb, SSD_BC // 2), lambda i: (i, 0)),
            pl.BlockSpec((tb, SSD_BC // 2), lambda i: (i, 0)),
        ],
        out_shape=[
            jax.ShapeDtypeStruct((n, SSD_INNER), BF16),
            jax.ShapeDtypeStruct((n, SSD_BC // 2), BF16),
            jax.ShapeDtypeStruct((n, SSD_BC // 2), BF16),
        ],
        scratch_shapes=[pltpu.VMEM((tb + 2 * hr, XBC_DIM), F32)],
        compiler_params=_params(("arbitrary",)),
        name="conv_silu",
    )(xbc, xbc3, xbc3, conv_w, conv_b)


def _ssd_body(xs_ref, bm_ref, cm_ref, dt_ref, par_ref, y_ref, st_ref, *, reverse, q):
    c = pl.program_id(1)

    @pl.when(c == 0)
    def _():
        st_ref[...] = jnp.zeros_like(st_ref)

    a = par_ref[0:1, :]
    bias = par_ref[1:2, :]
    dt = jax.nn.softplus(dt_ref[...] + bias)
    dta = dt * a
    ri = lax.broadcasted_iota(I32, (q, q), 0)
    ci = lax.broadcasted_iota(I32, (q, q), 1)
    mask = (ci >= ri) if reverse else (ci <= ri)
    cum = _dot_exact_lhs(mask.astype(BF16), dta)
    total = cum[0:1, :] if reverse else cum[q - 1:q, :]
    cum2 = cum * LOG2E
    tot2 = total * LOG2E
    rowp = cum2.T - jnp.log2(dt.T)
    ecum = jnp.exp2(cum2)
    wst = dt * jnp.exp2(tot2 - cum2)
    etot = jnp.exp2(tot2)
    lane = lax.broadcasted_iota(I32, (q, LANES), 1)
    lo = lane < SSD_HEADDIM
    hpg = SSD_HEADS // SSD_GROUPS
    gw = hpg * SSD_HEADDIM
    lane_g = lax.broadcasted_iota(I32, (1, gw), 1) // SSD_HEADDIM
    for g in range(SSD_GROUPS):
        bg = bm_ref[:, SSD_STATE * g:SSD_STATE * (g + 1)]
        cg = cm_ref[:, SSD_STATE * g:SSD_STATE * (g + 1)]
        cb = lax.dot_general(cg, bg, (((1,), (1,)), ((), ())), preferred_element_type=F32)
        st = st_ref[g]
        yoff = _dot(cg, st.astype(BF16))
        bgt = bg.astype(F32).T.astype(BF16)
        xw_parts = []
        for jp in range(hpg // 2):
            h0 = hpg * g + 2 * jp
            xt = xs_ref[:, SSD_HEADDIM * h0:SSD_HEADDIM * h0 + LANES]
            ws = []
            for hh in (h0, h0 + 1):
                seg = cum2[:, hh:hh + 1] - rowp[hh:hh + 1, :]
                e = jnp.exp2(jnp.where(mask, seg, -jnp.inf))
                ws.append((cb * e).astype(BF16))
            wp = jnp.concatenate(ws, axis=1)
            zero = jnp.zeros_like(xt)
            rhs = jnp.concatenate([jnp.where(lo, xt, zero), jnp.where(lo, zero, xt)], axis=0)
            yd = _dot(wp, rhs)
            ec = jnp.where(lo, ecum[:, h0:h0 + 1], ecum[:, h0 + 1:h0 + 2])
            y_ref[:, SSD_HEADDIM * h0:SSD_HEADDIM * h0 + LANES] = (
                yd + yoff[:, LANES * jp:LANES * (jp + 1)] * ec).astype(y_ref.dtype)
            wc = jnp.where(lo, wst[:, h0:h0 + 1], wst[:, h0 + 1:h0 + 2])
            xw_parts.append((xt.astype(F32) * wc).astype(BF16))
        xw = jnp.concatenate(xw_parts, axis=1)
        snew = _dot(bgt, xw)
        eg = jnp.zeros((1, gw), F32)
        for j in range(hpg):
            eg = jnp.where(lane_g == j, etot[:, hpg * g + j:hpg * g + j + 1], eg)
        st_ref[g] = st * eg + snew


def _ssd(xs, bm, cm, dt, par, batch, seq, reverse):
    n = xs.shape[0]
    q = SSD_Q
    nc = seq // q
    d = 1 if reverse else 0

    def blk(b, c):
        return b * nc + (nc - 1 - c if reverse else c)

    gw = (SSD_HEADS // SSD_GROUPS) * SSD_HEADDIM
    return pl.pallas_call(
        functools.partial(_ssd_body, reverse=reverse, q=q),
        grid=(batch, nc),
        in_specs=[
            pl.BlockSpec((q, SSD_INNER), lambda b, c: (blk(b, c), 0)),
            pl.BlockSpec((q, SSD_BC // 2), lambda b, c: (blk(b, c), 0)),
            pl.BlockSpec((q, SSD_BC // 2), lambda b, c: (blk(b, c), 0)),
            pl.BlockSpec((q, LANES), lambda b, c: (blk(b, c), d)),
            pl.BlockSpec((8, LANES), lambda b, c: (0, 0)),
        ],
        out_specs=pl.BlockSpec((q, SSD_INNER), lambda b, c: (blk(b, c), 0)),
        out_shape=jax.ShapeDtypeStruct((n, SSD_INNER), BF16),
        scratch_shapes=[pltpu.VMEM((SSD_GROUPS, SSD_STATE, gw), F32)],
        compiler_params=_params(("arbitrary", "arbitrary")),
        name="ssd_bwd" if reverse else "ssd_fwd",
    )(xs, bm, cm, dt, par)


def _s5_mats(a_re, a_im, log_step, b_re, b_im, c_re, c_im, reverse):
    t = S5_CHUNK
    hp = lax.Precision.HIGHEST
    step = jnp.exp(log_step.astype(F32))[:, None]
    lr = a_re.astype(F32) * step
    li = a_im.astype(F32) * step
    pw = jnp.arange(t + 1, dtype=F32)[None, :, None]
    mag = jnp.exp(lr[:, None, :] * pw)
    pr = mag * jnp.cos(li[:, None, :] * pw)
    pi = mag * jnp.sin(li[:, None, :] * pw)
    ar, ai = pr[:, 1], pi[:, 1]
    den = a_re.astype(F32) ** 2 + a_im.astype(F32) ** 2
    qr = ((ar - 1.0) * a_re + ai * a_im) / den
    qi = (ai * a_re - (ar - 1.0) * a_im) / den
    br = qr[..., None] * b_re - qi[..., None] * b_im
    bi = qr[..., None] * b_im + qi[..., None] * b_re
    cpr = c_re[:, None] * pr[:, :t, None, :] - c_im[:, None] * pi[:, :t, None, :]
    cpi = c_re[:, None] * pi[:, :t, None, :] + c_im[:, None] * pr[:, :t, None, :]
    m = jnp.einsum("gtkp,gpj->gtkj", cpr, br, precision=hp) - jnp.einsum("gtkp,gpj->gtkj", cpi, bi, precision=hp)
    s_i = np.arange(t)[:, None, None]
    t_i = np.arange(t)[None, :, None]
    lag = np.arange(t)[None, None, :]
    pick = ((s_i - t_i) if reverse else (t_i - s_i)) == lag
    mt = jnp.einsum("stu,gukj->gsjtk", jnp.asarray(pick, F32), m, precision=hp)
    toep = mt.reshape(S5_GROUPS, t * S5_GROUP, t * S5_GROUP)
    e_in = (jnp.arange(t) if reverse else (t - 1 - jnp.arange(t)))
    wr = pr[:, e_in][..., None] * br[:, None] - pi[:, e_in][..., None] * bi[:, None]
    wi = pr[:, e_in][..., None] * bi[:, None] + pi[:, e_in][..., None] * br[:, None]
    win_re = wr.transpose(0, 1, 3, 2).reshape(S5_GROUPS, t * S5_GROUP, S5_STATE)
    win_im = wi.transpose(0, 1, 3, 2).reshape(S5_GROUPS, t * S5_GROUP, S5_STATE)
    e_out = (t - jnp.arange(t)) if reverse else (jnp.arange(t) + 1)
    gr = c_re[:, None] * pr[:, e_out, None, :] - c_im[:, None] * pi[:, e_out, None, :]
    gi = c_re[:, None] * pi[:, e_out, None, :] + c_im[:, None] * pr[:, e_out, None, :]
    wout_re = gr.transpose(0, 3, 1, 2).reshape(S5_GROUPS, S5_STATE, t * S5_GROUP)
    wout_im = (-gi).transpose(0, 3, 1, 2).reshape(S5_GROUPS, S5_STATE, t * S5_GROUP)
    return toep, win_re, win_im, wout_re, wout_im, (lr, li)


def _pair_in(w):
    g, r, p = w.shape
    w4 = w.reshape(g // 2, 2, r, p)
    eye = jnp.eye(2, dtype=w.dtype)
    return (w4[:, :, :, None, :] * eye[None, :, None, :, None]).reshape(g // 2, 2 * r, 2 * p)


def _cmul(xr, xi, ar, ai):
    return xr * ar - xi * ai, xr * ai + xi * ar


def _s5_body(u_ref, toep_ref, win_ref, wout_ref, co_ref, y_ref, s_ref, x_ref, *, nseg_rows):
    u0 = u_ref[0]
    u1 = u_ref[1]
    up = jnp.concatenate([u0, u1], axis=1)
    for k in range(4):
        s_ref[k] = _dot(up, win_ref[0, k])
    co = co_ref[0]
    nr = nseg_rows
    zero = jnp.zeros((S5_SEGS, LANES), F32)

    def scan(kr, ki, ar, ai, asr, asi, reverse):
        def rows(i):
            r = (nr - 1 - i) if reverse else i
            return pl.ds(pl.multiple_of(r * S5_SEGS, S5_SEGS), S5_SEGS)

        def p1(i, c):
            xr, xi = _cmul(c[0], c[1], ar, ai)
            return xr + s_ref[kr, rows(i), :], xi + s_ref[ki, rows(i), :]

        er, ei = lax.fori_loop(0, nr, p1, (zero, zero))
        order = range(S5_SEGS - 1, -1, -1) if reverse else range(S5_SEGS)
        cr = jnp.zeros((1, LANES), F32)
        ci = jnp.zeros((1, LANES), F32)
        crs, cis = [None] * S5_SEGS, [None] * S5_SEGS
        for j in order:
            crs[j], cis[j] = cr, ci
            nr_, ni_ = _cmul(cr, ci, asr, asi)
            cr, ci = nr_ + er[j:j + 1], ni_ + ei[j:j + 1]
        c0 = (jnp.concatenate(crs, axis=0), jnp.concatenate(cis, axis=0))

        def p2(i, c):
            x_ref[kr, rows(i), :] = c[0]
            x_ref[ki, rows(i), :] = c[1]
            xr, xi = _cmul(c[0], c[1], ar, ai)
            return xr + s_ref[kr, rows(i), :], xi + s_ref[ki, rows(i), :]

        lax.fori_loop(0, nr, p2, c0)

    scan(0, 1, co[0:1], co[1:2], co[2:3], co[3:4], False)
    scan(2, 3, co[4:5], co[5:6], co[6:7], co[7:8], True)
    yc = _dot(x_ref[0].astype(BF16), wout_ref[0, 0])
    for k in range(1, 4):
        yc = yc + _dot(x_ref[k].astype(BF16), wout_ref[0, k])
    w = u0.shape[1]
    y_ref[0] = (_dot(u0, toep_ref[0, 0]) + yc[:, :w]).astype(y_ref.dtype)
    y_ref[1] = (_dot(u1, toep_ref[0, 1]) + yc[:, w:]).astype(y_ref.dtype)


def _s5(u_l, toep, win, wout, coef, batch, nc):
    g, _, w = u_l.shape
    npair = g // 2
    return pl.pallas_call(
        functools.partial(_s5_body, nseg_rows=nc // S5_SEGS),
        grid=(batch, npair),
        in_specs=[
            pl.BlockSpec((2, nc, w), lambda b, p: (p, b, 0)),
            pl.BlockSpec((1, 2, w, w), lambda b, p: (p, 0, 0, 0)),
            pl.BlockSpec((1, 4, 2 * w, LANES), lambda b, p: (p, 0, 0, 0)),
            pl.BlockSpec((1, 4, LANES, 2 * w), lambda b, p: (p, 0, 0, 0)),
            pl.BlockSpec((1, 8, LANES), lambda b, p: (p, 0, 0)),
        ],
        out_specs=pl.BlockSpec((2, nc, w), lambda b, p: (p, b, 0)),
        out_shape=jax.ShapeDtypeStruct(u_l.shape, BF16),
        scratch_shapes=[pltpu.VMEM((4, nc, LANES), F32), pltpu.VMEM((4, nc, LANES), F32)],
        compiler_params=_params(("arbitrary", "arbitrary")),
        name="s5_scan",
    )(u_l, toep, win, wout, coef)


def _layer_norm(r, g, b):
    mu = jnp.mean(r, axis=-1, keepdims=True)
    d = r - mu
    var = jnp.mean(d * d, axis=-1, keepdims=True)
    return d * lax.rsqrt(var + LN_EPS) * g + b


def _mid_body(yf_ref, yb_ref, xs_ref, z_ref, s5_ref, u_ref, g_ref, x_ref, dexp_ref, ng_ref, wout_ref, s5d_ref,
              wga_ref, wgb_ref, wo_ref, l1g_ref, l1b_ref, wr_ref, h_ref, lg_ref):
    y = yf_ref[...].astype(F32) + yb_ref[...].astype(F32) + xs_ref[...].astype(F32) * dexp_ref[...]
    z = z_ref[...].astype(F32)
    y = y * (z * _sigmoid(z))
    gw = SSD_INNER // SSD_GROUPS
    parts = []
    for g in range(SSD_GROUPS):
        yg = y[:, gw * g:gw * (g + 1)]
        ms = jnp.mean(yg * yg, axis=-1, keepdims=True)
        parts.append(yg * lax.rsqrt(ms + RMS_EPS))
    yn = (jnp.concatenate(parts, axis=1) * ng_ref[...]).astype(BF16)
    ya = _dot(yn, wout_ref[...])
    v = s5_ref[...].astype(F32) + u_ref[...].astype(F32) * s5d_ref[...]
    v = jax.nn.gelu(v).astype(BF16)
    yb = _dot(v, wga_ref[...]) * _sigmoid(_dot(v, wgb_ref[...]))
    gt = g_ref[...].astype(F32)
    mix = _sigmoid(gt[:, :D_MODEL]) * ya + _sigmoid(gt[:, D_MODEL:]) * yb
    r = DEEPNORM_ALPHA * x_ref[...] + _dot(mix.astype(BF16), wo_ref[...])
    h = _layer_norm(r, l1g_ref[...], l1b_ref[...])
    h_ref[...] = h
    hh = h.astype(BF16)
    hl = (h - hh.astype(F32)).astype(BF16)
    d = _dot(hh, wr_ref[...])
    lg_ref[...] = d[:, :LANES] + d[:, LANES:] + _dot(hl, wr_ref[:, :LANES])


def _mid(yf, yb, xs, z, s5y, u, gates, x2, dexp, ng, wout, s5d, wga, wgb, wo, l1g, l1b, wr, tb):
    n = x2.shape[0]
    row = lambda a: pl.BlockSpec((tb, a.shape[1]), lambda i: (i, 0))
    full = lambda a: pl.BlockSpec(a.shape, lambda i: (0, 0))
    acts = [yf, yb, xs, z, s5y, u, gates, x2]
    wts = [dexp, ng, wout, s5d, wga, wgb, wo, l1g, l1b, wr]
    return pl.pallas_call(
        _mid_body,
        grid=(n // tb,),
        in_specs=[row(a) for a in acts] + [full(w) for w in wts],
        out_specs=[pl.BlockSpec((tb, D_MODEL), lambda i: (i, 0)), pl.BlockSpec((tb, LANES), lambda i: (i, 0))],
        out_shape=[jax.ShapeDtypeStruct((n, D_MODEL), F32), jax.ShapeDtypeStruct((n, LANES), F32)],
        compiler_params=_params(("arbitrary",)),
        name="merge_ln1",
    )(*acts, *wts)


def _topk_body(lg_ref, gsel_ref, pos_ref, base_ref, aff_ref, *, seq, cap, tb):
    lane = lax.broadcasted_iota(I32, (seq, LANES), 1)
    valid = lane < N_EXPERTS
    lg = jnp.where(valid, lg_ref[...], -jnp.inf)
    m = jnp.max(lg, axis=1, keepdims=True)
    e = jnp.exp(lg - m)
    aff_ref[...] = e / jnp.sum(e, axis=1, keepdims=True)

    def count_ge(t_bits):
        t = pltpu.bitcast(t_bits, F32)
        return jnp.sum((aff_ref[...] >= t).astype(I32), axis=0, keepdims=True)

    def bs(_, c):
        lo, hi = c
        mid = lo + ((hi - lo) >> 1)
        ge = count_ge(mid) >= cap
        return jnp.where(ge, mid, lo), jnp.where(ge, hi, mid)

    lo0 = jnp.zeros((1, LANES), I32)
    hi0 = jnp.full((1, LANES), 0x3F800001, I32)
    thr_bits, _ = lax.fori_loop(0, 31, bs, (lo0, hi0))
    thr = pltpu.bitcast(thr_bits, F32)
    nxt = pltpu.bitcast(thr_bits + 1, F32)
    n_gt = count_ge(thr_bits + 1)
    need = (cap - n_gt).astype(F32)
    ri = lax.broadcasted_iota(I32, (tb, tb), 0)
    ci = lax.broadcasted_iota(I32, (tb, tb), 1)
    tri = (ci < ri).astype(BF16)
    vrow = lax.broadcasted_iota(I32, (1, LANES), 1) < N_EXPERTS

    def blk(j, c):
        ceq, csel = c
        rows = pl.ds(pl.multiple_of(j * tb, tb), tb)
        a = aff_ref[rows, :]
        gt = a >= nxt
        eq = (a >= thr) & jnp.logical_not(gt)
        eqc = ceq + _dot(tri, eq.astype(BF16))
        sel = (gt | (eq & (eqc < need))) & vrow
        self = sel.astype(F32)
        pos = csel + _dot(tri, self.astype(BF16))
        pos_ref[rows, :] = jnp.where(sel, pos, -1.0).astype(I32)
        gsel_ref[rows, :] = jnp.where(sel, a, 0.0)
        base_ref[0, pl.ds(j, 1), :] = csel.astype(I32)
        return (ceq + jnp.sum(eq.astype(F32), axis=0, keepdims=True),
                csel + jnp.sum(self, axis=0, keepdims=True))

    z = jnp.zeros((1, LANES), F32)
    lax.fori_loop(0, seq // tb, blk, (z, z))


def _topk(lg, batch, seq, cap, tb):
    nb = seq // tb
    return pl.pallas_call(
        functools.partial(_topk_body, seq=seq, cap=cap, tb=tb),
        grid=(batch,),
        in_specs=[pl.BlockSpec((seq, LANES), lambda b: (b, 0))],
        out_specs=[
            pl.BlockSpec((seq, LANES), lambda b: (b, 0)),
            pl.BlockSpec((seq, LANES), lambda b: (b, 0)),
            pl.BlockSpec((1, nb, LANES), lambda b: (b, 0, 0)),
        ],
        out_shape=[
            jax.ShapeDtypeStruct((batch * seq, LANES), F32),
            jax.ShapeDtypeStruct((batch * seq, LANES), I32),
            jax.ShapeDtypeStruct((batch, nb, LANES), I32),
        ],
        scratch_shapes=[pltpu.VMEM((seq, LANES), F32)],
        compiler_params=_params(("arbitrary",)),
        name="topk_select",
    )(lg)


def _compact_body(base_ref, pos_ref, idx_ref, *, nb, tb):
    b = pl.program_id(0)
    idx_ref[...] = jnp.zeros_like(idx_ref)
    lane = lax.broadcasted_iota(I32, (tb, LANES), 1)
    trow = lax.broadcasted_iota(I32, (tb, LANES), 0)

    def blk(j, carry):
        rows = pl.ds(pl.multiple_of(j * tb, tb), tb)
        p = pos_ref[rows, :]
        tok = trow + j * tb
        for e in range(N_EXPERTS):
            wb = base_ref[(b * nb + j) * N_EXPERTS + e] // LANES
            rel = p[:, e:e + 1] - wb * LANES
            lo = jnp.sum(jnp.where(rel == lane, tok, 0), axis=0, keepdims=True)
            hi = jnp.sum(jnp.where(rel - LANES == lane, tok, 0), axis=0, keepdims=True)
            idx_ref[0, e, pl.ds(wb, 1), :] += lo
            idx_ref[0, e, pl.ds(wb + 1, 1), :] += hi
        return carry

    lax.fori_loop(0, nb, blk, 0)


def _compact(base_flat, pos, batch, seq, cap, tb):
    nb = seq // tb
    rows = cap // LANES + 8
    return pl.pallas_call(
        functools.partial(_compact_body, nb=nb, tb=tb),
        grid_spec=pltpu.PrefetchScalarGridSpec(
            num_scalar_prefetch=1,
            grid=(batch,),
            in_specs=[pl.BlockSpec((seq, LANES), lambda b, base: (b, 0))],
            out_specs=pl.BlockSpec((1, N_EXPERTS, rows, LANES), lambda b, base: (b, 0, 0, 0)),
        ),
        out_shape=jax.ShapeDtypeStruct((batch, N_EXPERTS, rows, LANES), I32),
        compiler_params=_params(("arbitrary",)),
        name="slot_compact",
    )(base_flat, pos)


def _ffn_body(idx_ref, idxn_ref, h3_ref, w1_ref, w3_ref, w2_ref, y_ref, xbuf_ref, xb_ref, acc_ref, sem,
              *, nf, nbatch, seq, cap):
    e = pl.program_id(0)
    b = pl.program_id(1)
    f = pl.program_id(2)
    eb = e * nbatch + b
    neb = pl.num_programs(0) * nbatch
    slot = eb % 2
    nk = D_MODEL // LANES

    def issue(iref, tok0, sl):
        def body(s, c):
            t = iref[0, 0, s] + tok0
            pltpu.make_async_copy(h3_ref.at[t], xbuf_ref.at[sl, :, s, :], sem.at[sl]).start()
            return c

        lax.fori_loop(0, cap, body, 0, unroll=8)

    @pl.when((f == 0) & (eb == 0))
    def _():
        issue(idx_ref, b * seq, slot)

    @pl.when(f == 0)
    def _():
        pltpu.make_async_copy(xbuf_ref.at[slot], xbuf_ref.at[slot], sem.at[slot]).wait()
        for k in range(nk):
            xb_ref[:, LANES * k:LANES * (k + 1)] = xbuf_ref[slot, k].astype(BF16)

    @pl.when((f == 1) & (eb + 1 < neb))
    def _():
        issue(idxn_ref, ((eb + 1) % nbatch) * seq, 1 - slot)

    x = xb_ref[...]
    a = _dot(x, w1_ref[0].astype(BF16))
    b = _dot(x, w3_ref[0].astype(BF16))
    hid = (a * _sigmoid(a) * b).astype(BF16)
    p = _dot(hid, w2_ref[0].astype(BF16))

    @pl.when(f == 0)
    def _():
        acc_ref[...] = p

    @pl.when(f > 0)
    def _():
        acc_ref[...] += p

    @pl.when(f == nf - 1)
    def _():
        y_ref[0, 0] = acc_ref[...].astype(y_ref.dtype)


def _ffn(idx3, h3, w1, w3, w2, batch, seq, cap, ft):
    ne, d, ff = w1.shape
    nf = ff // ft
    assert nf >= 2
    neb = ne * batch

    def nxt(e, b, f):
        eb1 = jnp.minimum(e * batch + b + 1, neb - 1)
        return ((eb1 % batch) * ne + eb1 // batch, 0, 0)

    return pl.pallas_call(
        functools.partial(_ffn_body, nf=nf, nbatch=batch, seq=seq, cap=cap),
        grid=(ne, batch, nf),
        in_specs=[
            pl.BlockSpec((1, 1, cap), lambda e, b, f: (b * ne + e, 0, 0), memory_space=pltpu.SMEM),
            pl.BlockSpec((1, 1, cap), nxt, memory_space=pltpu.SMEM),
            pl.BlockSpec(memory_space=pl.ANY),
            pl.BlockSpec((1, d, ft), lambda e, b, f: (e, 0, f)),
            pl.BlockSpec((1, d, ft), lambda e, b, f: (e, 0, f)),
            pl.BlockSpec((1, ft, d), lambda e, b, f: (e, f, 0)),
        ],
        out_specs=pl.BlockSpec((1, 1, cap, d), lambda e, b, f: (b, e, 0, 0)),
        out_shape=jax.ShapeDtypeStruct((batch, ne, cap, d), BF16),
        scratch_shapes=[
            pltpu.VMEM((2, d // LANES, cap, LANES), F32),
            pltpu.VMEM((cap, d), BF16),
            pltpu.VMEM((cap, d), F32),
            pltpu.SemaphoreType.DMA((2,)),
        ],
        compiler_params=_params(("arbitrary", "arbitrary", "arbitrary")),
        name="expert_ffn",
    )(idx3, idx3, h3, w1, w3, w2)


def _combine_body(base_ref, pos_ref, gsel_ref, h_ref, y_ref, g_ref, b_ref, o_ref, win_ref, sem,
                  *, nb, tb, cap, win, wsm):
    bi = pl.program_id(0)
    j = pl.program_id(1)
    step = bi * nb + j
    nsteps = pl.num_programs(0) * nb
    slot = step % 2

    def w0_of(s, e, w):
        base = base_ref[s * N_EXPERTS + e]
        return pl.multiple_of(jnp.minimum((base // 16) * 16, cap - w), 16)

    def fits_small(s):
        ok = None
        for e in range(N_EXPERTS):
            nxt = base_ref[jnp.minimum(s + 1, nsteps - 1) * N_EXPERTS + e]
            end = jnp.where(s % nb == nb - 1, cap, nxt)
            ok_e = end <= w0_of(s, e, wsm) + wsm
            ok = ok_e if ok is None else ok & ok_e
        return ok

    def copies(s, sl, w):
        b_s = s // nb
        return [pltpu.make_async_copy(y_ref.at[b_s, e, pl.ds(w0_of(s, e, w), w), :],
                                      win_ref.at[sl, e, pl.ds(0, w), :], sem.at[sl]) for e in range(N_EXPERTS)]

    def start(s, sl):
        small = fits_small(s)

        @pl.when(small)
        def _():
            for cp in copies(s, sl, wsm):
                cp.start()

        @pl.when(jnp.logical_not(small))
        def _():
            for cp in copies(s, sl, win):
                cp.start()

    @pl.when(step == 0)
    def _():
        start(step, slot)

    @pl.when(step + 1 < nsteps)
    def _():
        start(step + 1, 1 - slot)

    def finish(w):
        for cp in copies(step, slot, w):
            cp.wait()
        p = pos_ref[...]
        gs = gsel_ref[...]
        lane = lax.broadcasted_iota(I32, (tb, w), 1)
        acc = jnp.zeros((tb, D_MODEL), F32)
        for e in range(N_EXPERTS):
            rel = p[:, e:e + 1] - w0_of(step, e, w)
            s_e = jnp.where(rel == lane, gs[:, e:e + 1], 0.0).astype(BF16)
            acc = acc + _dot(s_e, win_ref[slot, e, 0:w, :])
        r = DEEPNORM_ALPHA * h_ref[...] + acc
        o_ref[...] = _layer_norm(r, g_ref[...], b_ref[...])

    small_now = fits_small(step)

    @pl.when(small_now)
    def _():
        finish(wsm)

    @pl.when(jnp.logical_not(small_now))
    def _():
        finish(win)


def _combine(base_flat, pos, gsel, h, y, g, b, batch, seq, cap, tb, wsm=48):
    nb = seq // tb
    win = tb + 16
    return pl.pallas_call(
        functools.partial(_combine_body, nb=nb, tb=tb, cap=cap, win=win, wsm=wsm),
        grid_spec=pltpu.PrefetchScalarGridSpec(
            num_scalar_prefetch=1,
            grid=(batch, nb),
            in_specs=[
                pl.BlockSpec((tb, LANES), lambda bi, j, base: (bi * nb + j, 0)),
                pl.BlockSpec((tb, LANES), lambda bi, j, base: (bi * nb + j, 0)),
                pl.BlockSpec((tb, D_MODEL), lambda bi, j, base: (bi * nb + j, 0)),
                pl.BlockSpec(memory_space=pl.ANY),
                pl.BlockSpec((1, D_MODEL), lambda bi, j, base: (0, 0)),
                pl.BlockSpec((1, D_MODEL), lambda bi, j, base: (0, 0)),
            ],
            out_specs=pl.BlockSpec((tb, D_MODEL), lambda bi, j, base: (bi * nb + j, 0)),
            scratch_shapes=[pltpu.VMEM((2, N_EXPERTS, win, D_MODEL), BF16), pltpu.SemaphoreType.DMA((2,))],
        ),
        out_shape=jax.ShapeDtypeStruct((batch * seq, D_MODEL), F32),
        compiler_params=_params(("arbitrary", "arbitrary")),
        name="combine_ln2",
    )(base_flat, pos, gsel, h, y, g, b)


def _pad_lanes(v, width=LANES):
    return jnp.pad(v, [(0, 0)] * (v.ndim - 1) + [(0, width - v.shape[-1])])


def _hi_lo(w):
    hi = w.astype(BF16)
    lo = (w - hi.astype(F32)).astype(BF16)
    return jnp.concatenate([hi, lo], axis=1)


def _layer(x, w_in, conv_w, conv_b, ssd_a_log_f, ssd_a_log_b, ssd_dt_bias_f, ssd_dt_bias_b, ssd_d, ssd_norm_g,
           w_out_ssd, s5_a_re_f, s5_a_im_f, s5_log_step_f, s5_a_re_b, s5_a_im_b, s5_log_step_b, s5_b_re, s5_b_im,
           s5_c_re, s5_c_im, s5_d, w_glu_a, w_glu_b, w_o, ln1_g, ln1_b, w_router, w_e1, w_e3, w_e2, ln2_g, ln2_b):
    batch, seq, d = x.shape
    n = batch * seq
    x2 = x.reshape(n, d)
    c0, c1, c2, c3 = SSD_INNER, SSD_INNER + XBC_DIM, SSD_INNER + XBC_DIM + 2 * SSD_HEADS, \
        SSD_INNER + XBC_DIM + 2 * SSD_HEADS + S5_WIDTH
    wdt = w_in[:, c1:c2]
    wdt = jnp.concatenate([_pad_lanes(wdt[:, :SSD_HEADS]), _pad_lanes(wdt[:, SSD_HEADS:])], axis=1)
    z, xbc, dt, u, gates = _in_proj(x2, w_in[:, :c0].astype(BF16), w_in[:, c0:c1].astype(BF16), _hi_lo(wdt),
                                    w_in[:, c2:c3].astype(BF16), w_in[:, c3:].astype(BF16), tb=256)

    t = S5_CHUNK
    nc = seq // t
    nr = nc // S5_SEGS
    u_l = u.astype(BF16).reshape(batch, S5_SEGS, nr, t, S5_GROUPS, S5_GROUP)
    u_l = u_l.transpose(4, 0, 2, 1, 3, 5).reshape(S5_GROUPS, batch * nc, t * S5_GROUP)
    mf = _s5_mats(s5_a_re_f, s5_a_im_f, s5_log_step_f, s5_b_re, s5_b_im, s5_c_re, s5_c_im, False)
    mb = _s5_mats(s5_a_re_b, s5_a_im_b, s5_log_step_b, s5_b_re, s5_b_im, s5_c_re, s5_c_im, True)
    npair = S5_GROUPS // 2
    toep = (mf[0] + mb[0]).astype(BF16).reshape(npair, 2, t * S5_GROUP, t * S5_GROUP)
    win = jnp.stack([_pair_in(mf[1]), _pair_in(mf[2]), _pair_in(mb[1]), _pair_in(mb[2])], axis=1).astype(BF16)
    wout = jnp.stack([_pair_in(m.transpose(0, 2, 1)).transpose(0, 2, 1) for m in (mf[3], mf[4], mb[3], mb[4])],
                     axis=1).astype(BF16)

    def powc(lr, li, k):
        mag = jnp.exp(lr * k)
        return (mag * jnp.cos(li * k)).reshape(npair, LANES), (mag * jnp.sin(li * k)).reshape(npair, LANES)

    coef = jnp.stack([*powc(*mf[5], float(t)), *powc(*mf[5], float(t * nr)),
                      *powc(*mb[5], float(t)), *powc(*mb[5], float(t * nr))], axis=1)
    s5y = _s5(u_l, toep, win, wout, coef, batch, nc)
    s5y = s5y.reshape(S5_GROUPS, batch, nr, S5_SEGS, t, S5_GROUP).transpose(1, 3, 2, 4, 0, 5).reshape(n, S5_WIDTH)

    xs, bm, cm = _conv(xbc, conv_w, conv_b.reshape(1, XBC_DIM), seq, tb=512)

    def ssd_par(a_log, bias):
        p = jnp.zeros((8, LANES), F32)
        return p.at[0, :SSD_HEADS].set(-jnp.exp(a_log)).at[1, :SSD_HEADS].set(bias)

    yb = _ssd(xs, bm, cm, dt, ssd_par(ssd_a_log_b, ssd_dt_bias_b), batch, seq, True)
    yf = _ssd(xs, bm, cm, dt, ssd_par(ssd_a_log_f, ssd_dt_bias_f), batch, seq, False)

    wr = _hi_lo(_pad_lanes(w_router))
    h, lg = _mid(yf, yb, xs, z, s5y, u, gates, x2, jnp.repeat(ssd_d, SSD_HEADDIM).reshape(1, SSD_INNER),
                 ssd_norm_g.reshape(1, SSD_INNER), w_out_ssd.astype(BF16), s5_d.reshape(1, S5_WIDTH),
                 w_glu_a.astype(BF16), w_glu_b.astype(BF16), w_o.astype(BF16), ln1_g.reshape(1, d),
                 ln1_b.reshape(1, d), wr, tb=256)

    cap = CAPACITY_FACTOR * seq // N_EXPERTS
    tbk = 128
    gsel, pos, base = _topk(lg, batch, seq, cap, tbk)
    base_flat = base[:, :, :N_EXPERTS].reshape(-1)
    idx = _compact(base_flat, pos, batch, seq, cap, tbk)
    idx3 = idx[:, :, :cap // LANES, :].reshape(batch * N_EXPERTS, 1, cap)
    y = _ffn(idx3, h.reshape(n, d // LANES, LANES), w_e1, w_e3, w_e2, batch, seq, cap, ft=256)
    out = _combine(base_flat, pos, gsel, h, y, ln2_g.reshape(1, d), ln2_b.reshape(1, d), batch, seq, cap, tbk)
    return out.reshape(batch, seq, d)


def kernel(x, w_in, conv_w, conv_b, ssd_a_log_f, ssd_a_log_b, ssd_dt_bias_f, ssd_dt_bias_b, ssd_d, ssd_norm_g, w_out_ssd, s5_a_re_f, s5_a_im_f, s5_log_step_f, s5_a_re_b, s5_a_im_b, s5_log_step_b, s5_b_re, s5_b_im, s5_c_re, s5_c_im, s5_d, w_glu_a, w_glu_b, w_o, ln1_g, ln1_b, w_router, w_e1, w_e3, w_e2, ln2_g, ln2_b):
    h = x
    for i in range(DEPTH):
        h = _layer(h, w_in[i], conv_w[i], conv_b[i], ssd_a_log_f[i], ssd_a_log_b[i], ssd_dt_bias_f[i],
                   ssd_dt_bias_b[i], ssd_d[i], ssd_norm_g[i], w_out_ssd[i], s5_a_re_f[i], s5_a_im_f[i],
                   s5_log_step_f[i], s5_a_re_b[i], s5_a_im_b[i], s5_log_step_b[i], s5_b_re[i], s5_b_im[i],
                   s5_c_re[i], s5_c_im[i], s5_d[i], w_glu_a[i], w_glu_b[i], w_o[i], ln1_g[i], ln1_b[i],
                   w_router[i], w_e1[i], w_e3[i], w_e2[i], ln2_g[i], ln2_b[i])
    return h
```

```python
import functools
import math

import jax
import jax.numpy as jnp
import numpy as np
from jax import lax
from jax.experimental import pallas as pl
from jax.experimental.pallas import tpu as pltpu

F32 = jnp.float32
BF16 = jnp.bfloat16
I32 = jnp.int32

D_MODEL = 1024
SSD_HEADDIM = 64
SSD_HEADS = 24
SSD_INNER = SSD_HEADS * SSD_HEADDIM
SSD_GROUPS = 4
SSD_STATE = 128
SSD_CONV = 5
SSD_BC = 2 * SSD_GROUPS * SSD_STATE
XBC_DIM = SSD_INNER + SSD_BC
S5_GROUP = 16
S5_WIDTH = 768
S5_GROUPS = S5_WIDTH // S5_GROUP
S5_STATE = 64
N_EXPERTS = 16
EXPERT_FF = 2816
CAPACITY_FACTOR = 2
DEPTH = 1
DEEPNORM_ALPHA = (2.0 * DEPTH) ** 0.25
LN_EPS = 1e-5
RMS_EPS = 1e-5
LOG2E = 1.4426950408889634

LANES = 128
S5_CHUNK = 16
S5_SEGS = 8
SSD_Q = 128
CONV_HALO = 16
VMEM_LIMIT = 56 * 1024 * 1024


def _dot(a, b):
    return jnp.dot(a, b, preferred_element_type=F32)


def _split3(v):
    v1 = v.astype(BF16)
    r1 = v - v1.astype(F32)
    v2 = r1.astype(BF16)
    v3 = (r1 - v2.astype(F32)).astype(BF16)
    return v1, v2, v3


def _dot_exact_lhs(m_bf16, v):
    v1, v2, v3 = _split3(v)
    return _dot(m_bf16, v1) + _dot(m_bf16, v2) + _dot(m_bf16, v3)


def _sigmoid(x):
    return 1.0 / (1.0 + jnp.exp(-x))


def _params(sem, **kw):
    return pltpu.CompilerParams(dimension_semantics=sem, vmem_limit_bytes=VMEM_LIMIT, **kw)


def _in_proj_body(x_ref, wz_ref, wxbc_ref, wdt_ref, wu_ref, wg_ref, z_ref, xbc_ref, dt_ref, u_ref, g_ref):
    x = x_ref[...]
    xh = x.astype(BF16)
    xl = (x - xh.astype(F32)).astype(BF16)
    z_ref[...] = _dot(xh, wz_ref[...]).astype(z_ref.dtype)
    xbc_ref[...] = _dot(xh, wxbc_ref[...]).astype(xbc_ref.dtype)
    u_ref[...] = _dot(xh, wu_ref[...]).astype(u_ref.dtype)
    g_ref[...] = _dot(xh, wg_ref[...]).astype(g_ref.dtype)
    d = _dot(xh, wdt_ref[...])
    nd = dt_ref.shape[-1]
    dt_ref[...] = d[:, :nd] + d[:, nd:] + _dot(xl, wdt_ref[:, :nd])


def _in_proj(x2, wz, wxbc, wdt, wu, wg, tb):
    n = x2.shape[0]
    full = lambda w: pl.BlockSpec(w.shape, lambda i: (0, 0))
    row = lambda c: pl.BlockSpec((tb, c), lambda i: (i, 0))
    nd = wdt.shape[1] // 2
    return pl.pallas_call(
        _in_proj_body,
        grid=(n // tb,),
        in_specs=[row(D_MODEL), full(wz), full(wxbc), full(wdt), full(wu), full(wg)],
        out_specs=[row(SSD_INNER), row(XBC_DIM), row(nd), row(S5_WIDTH), row(2 * D_MODEL)],
        out_shape=[
            jax.ShapeDtypeStruct((n, SSD_INNER), BF16),
            jax.ShapeDtypeStruct((n, XBC_DIM), BF16),
            jax.ShapeDtypeStruct((n, nd), F32),
            jax.ShapeDtypeStruct((n, S5_WIDTH), BF16),
            jax.ShapeDtypeStruct((n, 2 * D_MODEL), BF16),
        ],
        compiler_params=_params(("arbitrary",)),
        name="in_proj",
    )(x2, wz, wxbc, wdt, wu, wg)


def _conv_body(cur_ref, prev_ref, next_ref, w_ref, b_ref, xs_ref, bm_ref, cm_ref, ext_ref, *, tb, blocks_per_seq):
    i = pl.program_id(0)
    first = (i % blocks_per_seq) == 0
    last = (i % blocks_per_seq) == blocks_per_seq - 1
    hr = CONV_HALO
    ext_ref[0:hr, :] = jnp.where(first, 0.0, prev_ref[0].astype(F32))
    ext_ref[hr:hr + tb, :] = cur_ref[...].astype(F32)
    ext_ref[hr + tb:2 * hr + tb, :] = jnp.where(last, 0.0, next_ref[0].astype(F32))
    half = SSD_CONV // 2
    cw = 512
    for c0 in range(0, XBC_DIM, cw):
        acc = jnp.broadcast_to(b_ref[:, c0:c0 + cw], (tb, cw))
        for j in range(SSD_CONV):
            acc = acc + ext_ref[hr - half + j:hr - half + j + tb, c0:c0 + cw] * w_ref[j:j + 1, c0:c0 + cw]
        y = (acc * _sigmoid(acc)).astype(BF16)
        if c0 + cw <= SSD_INNER:
            xs_ref[:, c0:c0 + cw] = y
        elif c0 < SSD_INNER + SSD_BC // 2:
            bm_ref[:, c0 - SSD_INNER:c0 - SSD_INNER + cw] = y
        else:
            o = c0 - SSD_INNER - SSD_BC // 2
            cm_ref[:, o:o + cw] = y


def _conv(xbc, conv_w, conv_b, seq, tb):
    n = xbc.shape[0]
    nb = n // tb
    hr = CONV_HALO
    xbc3 = xbc.reshape(n // hr, hr, XBC_DIM)
    r = tb // hr
    return pl.pallas_call(
        functools.partial(_conv_body, tb=tb, blocks_per_seq=seq // tb),
        grid=(nb,),
        in_specs=[
            pl.BlockSpec((tb, XBC_DIM), lambda i: (i, 0)),
            pl.BlockSpec((1, hr, XBC_DIM), lambda i: (jnp.maximum(i * r - 1, 0), 0, 0)),
            pl.BlockSpec((1, hr, XBC_DIM), lambda i: (jnp.minimum((i + 1) * r, n // hr - 1), 0, 0)),
            pl.BlockSpec((SSD_CONV, XBC_DIM), lambda i: (0, 0)),
            pl.BlockSpec((1, XBC_DIM), lambda i: (0, 0)),
        ],
        out_specs=[
            pl.BlockSpec((tb, SSD_INNER), lambda i: (i, 0)),
            pl.BlockSpec((tb, SSD_BC // 2), lambda i: (i, 0)),
            pl.BlockSpec((tb, SSD_BC // 2), lambda i: (i, 0)),
        ],
        out_shape=[
            jax.ShapeDtypeStruct((n, SSD_INNER), BF16),
            jax.ShapeDtypeStruct((n, SSD_BC // 2), BF16),
            jax.ShapeDtypeStruct((n, SSD_BC // 2), BF16),
        ],
        scratch_shapes=[pltpu.VMEM((tb + 2 * hr, XBC_DIM), F32)],
        compiler_params=_params(("arbitrary",)),
        name="conv_silu",
    )(xbc, xbc3, xbc3, conv_w, conv_b)


def _ssd_body(xs_ref, bm_ref, cm_ref, dt_ref, par_ref, y_ref, st_ref, *, reverse, q):
    c = pl.program_id(1)

    @pl.when(c == 0)
    def _():
        st_ref[...] = jnp.zeros_like(st_ref)

    a = par_ref[0:1, :]
    bias = par_ref[1:2, :]
    dt = jax.nn.softplus(dt_ref[...] + bias)
    dta = dt * a
    ri = lax.broadcasted_iota(I32, (q, q), 0)
    ci = lax.broadcasted_iota(I32, (q, q), 1)
    mask = (ci >= ri) if reverse else (ci <= ri)
    cum = _dot_exact_lhs(mask.astype(BF16), dta)
    total = cum[0:1, :] if reverse else cum[q - 1:q, :]
    cum2 = cum * LOG2E
    tot2 = total * LOG2E
    rowp = cum2.T - jnp.log2(dt.T)
    ecum = jnp.exp2(cum2)
    wst = dt * jnp.exp2(tot2 - cum2)
    etot = jnp.exp2(tot2)
    lane = lax.broadcasted_iota(I32, (q, LANES), 1)
    lo = lane < SSD_HEADDIM
    hpg = SSD_HEADS // SSD_GROUPS
    gw = hpg * SSD_HEADDIM
    lane_g = lax.broadcasted_iota(I32, (1, gw), 1) // SSD_HEADDIM
    for g in range(SSD_GROUPS):
        bg = bm_ref[:, SSD_STATE * g:SSD_STATE * (g + 1)]
        cg = cm_ref[:, SSD_STATE * g:SSD_STATE * (g + 1)]
        cb = lax.dot_general(cg, bg, (((1,), (1,)), ((), ())), preferred_element_type=F32)
        st = st_ref[g]
        yoff = _dot(cg, st.astype(BF16))
        bgt = bg.astype(F32).T.astype(BF16)
        xw_parts = []
        for jp in range(hpg // 2):
            h0 = hpg * g + 2 * jp
            xt = xs_ref[:, SSD_HEADDIM * h0:SSD_HEADDIM * h0 + LANES]
            ws = []
            for hh in (h0, h0 + 1):
                seg = cum2[:, hh:hh + 1] - rowp[hh:hh + 1, :]
                e = jnp.exp2(jnp.where(mask, seg, -jnp.inf))
                ws.append((cb * e).astype(BF16))
            wp = jnp.concatenate(ws, axis=1)
            zero = jnp.zeros_like(xt)
            rhs = jnp.concatenate([jnp.where(lo, xt, zero), jnp.where(lo, zero, xt)], axis=0)
            yd = _dot(wp, rhs)
            ec = jnp.where(lo, ecum[:, h0:h0 + 1], ecum[:, h0 + 1:h0 + 2])
            y_ref[:, SSD_HEADDIM * h0:SSD_HEADDIM * h0 + LANES] = (
                yd + yoff[:, LANES * jp:LANES * (jp + 1)] * ec).astype(y_ref.dtype)
            wc = jnp.where(lo, wst[:, h0:h0 + 1], wst[:, h0 + 1:h0 + 2])
            xw_parts.append((xt.astype(F32) * wc).astype(BF16))
        xw = jnp.concatenate(xw_parts, axis=1)
        snew = _dot(bgt, xw)
        eg = jnp.zeros((1, gw), F32)
        for j in range(hpg):
            eg = jnp.where(lane_g == j, etot[:, hpg * g + j:hpg * g + j + 1], eg)
        st_ref[g] = st * eg + snew


def _ssd(xs, bm, cm, dt, par, batch, seq, reverse):
    n = xs.shape[0]
    q = SSD_Q
    nc = seq // q
    d = 1 if reverse else 0

    def blk(b, c):
        return b * nc + (nc - 1 - c if reverse else c)

    gw = (SSD_HEADS // SSD_GROUPS) * SSD_HEADDIM
    return pl.pallas_call(
        functools.partial(_ssd_body, reverse=reverse, q=q),
        grid=(batch, nc),
        in_specs=[
            pl.BlockSpec((q, SSD_INNER), lambda b, c: (blk(b, c), 0)),
            pl.BlockSpec((q, SSD_BC // 2), lambda b, c: (blk(b, c), 0)),
            pl.BlockSpec((q, SSD_BC // 2), lambda b, c: (blk(b, c), 0)),
            pl.BlockSpec((q, LANES), lambda b, c: (blk(b, c), d)),
            pl.BlockSpec((8, LANES), lambda b, c: (0, 0)),
        ],
        out_specs=pl.BlockSpec((q, SSD_INNER), lambda b, c: (blk(b, c), 0)),
        out_shape=jax.ShapeDtypeStruct((n, SSD_INNER), BF16),
        scratch_shapes=[pltpu.VMEM((SSD_GROUPS, SSD_STATE, gw), F32)],
        compiler_params=_params(("arbitrary", "arbitrary")),
        name="ssd_bwd" if reverse else "ssd_fwd",
    )(xs, bm, cm, dt, par)


def _s5_mats(a_re, a_im, log_step, b_re, b_im, c_re, c_im, reverse):
    t = S5_CHUNK
    hp = lax.Precision.HIGHEST
    step = jnp.exp(log_step.astype(F32))[:, None]
    lr = a_re.astype(F32) * step
    li = a_im.astype(F32) * step
    pw = jnp.arange(t + 1, dtype=F32)[None, :, None]
    mag = jnp.exp(lr[:, None, :] * pw)
    pr = mag * jnp.cos(li[:, None, :] * pw)
    pi = mag * jnp.sin(li[:, None, :] * pw)
    ar, ai = pr[:, 1], pi[:, 1]
    den = a_re.astype(F32) ** 2 + a_im.astype(F32) ** 2
    qr = ((ar - 1.0) * a_re + ai * a_im) / den
    qi = (ai * a_re - (ar - 1.0) * a_im) / den
    br = qr[..., None] * b_re - qi[..., None] * b_im
    bi = qr[..., None] * b_im + qi[..., None] * b_re
    cpr = c_re[:, None] * pr[:, :t, None, :] - c_im[:, None] * pi[:, :t, None, :]
    cpi = c_re[:, None] * pi[:, :t, None, :] + c_im[:, None] * pr[:, :t, None, :]
    m = jnp.einsum("gtkp,gpj->gtkj", cpr, br, precision=hp) - jnp.einsum("gtkp,gpj->gtkj", cpi, bi, precision=hp)
    s_i = np.arange(t)[:, None, None]
    t_i = np.arange(t)[None, :, None]
    lag = np.arange(t)[None, None, :]
    pick = ((s_i - t_i) if reverse else (t_i - s_i)) == lag
    mt = jnp.einsum("stu,gukj->gsjtk", jnp.asarray(pick, F32), m, precision=hp)
    toep = mt.reshape(S5_GROUPS, t * S5_GROUP, t * S5_GROUP)
    e_in = (jnp.arange(t) if reverse else (t - 1 - jnp.arange(t)))
    wr = pr[:, e_in][..., None] * br[:, None] - pi[:, e_in][..., None] * bi[:, None]
    wi = pr[:, e_in][..., None] * bi[:, None] + pi[:, e_in][..., None] * br[:, None]
    win_re = wr.transpose(0, 1, 3, 2).reshape(S5_GROUPS, t * S5_GROUP, S5_STATE)
    win_im = wi.transpose(0, 1, 3, 2).reshape(S5_GROUPS, t * S5_GROUP, S5_STATE)
    e_out = (t - jnp.arange(t)) if reverse else (jnp.arange(t) + 1)
    gr = c_re[:, None] * pr[:, e_out, None, :] - c_im[:, None] * pi[:, e_out, None, :]
    gi = c_re[:, None] * pi[:, e_out, None, :] + c_im[:, None] * pr[:, e_out, None, :]
    wout_re = gr.transpose(0, 3, 1, 2).reshape(S5_GROUPS, S5_STATE, t * S5_GROUP)
    wout_im = (-gi).transpose(0, 3, 1, 2).reshape(S5_GROUPS, S5_STATE, t * S5_GROUP)
    return toep, win_re, win_im, wout_re, wout_im, (lr, li)


def _pair_in(w):
    g, r, p = w.shape
    w4 = w.reshape(g // 2, 2, r, p)
    eye = jnp.eye(2, dtype=w.dtype)
    return (w4[:, :, :, None, :] * eye[None, :, None, :, None]).reshape(g // 2, 2 * r, 2 * p)


def _cmul(xr, xi, ar, ai):
    return xr * ar - xi * ai, xr * ai + xi * ar


def _s5_body(u_ref, toep_ref, win_ref, wout_ref, co_ref, y_ref, s_ref, x_ref, *, nseg_rows):
    u0 = u_ref[0]
    u1 = u_ref[1]
    up = jnp.concatenate([u0, u1], axis=1)
    for k in range(4):
        s_ref[k] = _dot(up, win_ref[0, k])
    co = co_ref[0]
    nr = nseg_rows
    zero = jnp.zeros((S5_SEGS, LANES), F32)

    def scan(kr, ki, ar, ai, asr, asi, reverse):
        def rows(i):
            r = (nr - 1 - i) if reverse else i
            return pl.ds(pl.multiple_of(r * S5_SEGS, S5_SEGS), S5_SEGS)

        def p1(i, c):
            xr, xi = _cmul(c[0], c[1], ar, ai)
            return xr + s_ref[kr, rows(i), :], xi + s_ref[ki, rows(i), :]

        er, ei = lax.fori_loop(0, nr, p1, (zero, zero))
        order = range(S5_SEGS - 1, -1, -1) if reverse else range(S5_SEGS)
        cr = jnp.zeros((1, LANES), F32)
        ci = jnp.zeros((1, LANES), F32)
        crs, cis = [None] * S5_SEGS, [None] * S5_SEGS
        for j in order:
            crs[j], cis[j] = cr, ci
            nr_, ni_ = _cmul(cr, ci, asr, asi)
            cr, ci = nr_ + er[j:j + 1], ni_ + ei[j:j + 1]
        c0 = (jnp.concatenate(crs, axis=0), jnp.concatenate(cis, axis=0))

        def p2(i, c):
            x_ref[kr, rows(i), :] = c[0]
            x_ref[ki, rows(i), :] = c[1]
            xr, xi = _cmul(c[0], c[1], ar, ai)
            return xr + s_ref[kr, rows(i), :], xi + s_ref[ki, rows(i), :]

        lax.fori_loop(0, nr, p2, c0)

    scan(0, 1, co[0:1], co[1:2], co[2:3], co[3:4], False)
    scan(2, 3, co[4:5], co[5:6], co[6:7], co[7:8], True)
    yc = _dot(x_ref[0].astype(BF16), wout_ref[0, 0])
    for k in range(1, 4):
        yc = yc + _dot(x_ref[k].astype(BF16), wout_ref[0, k])
    w = u0.shape[1]
    y_ref[0] = (_dot(u0, toep_ref[0, 0]) + yc[:, :w]).astype(y_ref.dtype)
    y_ref[1] = (_dot(u1, toep_ref[0, 1]) + yc[:, w:]).astype(y_ref.dtype)


def _s5(u_l, toep, win, wout, coef, batch, nc):
    g, _, w = u_l.shape
    npair = g // 2
    return pl.pallas_call(
        functools.partial(_s5_body, nseg_rows=nc // S5_SEGS),
        grid=(batch, npair),
        in_specs=[
            pl.BlockSpec((2, nc, w), lambda b, p: (p, b, 0)),
            pl.BlockSpec((1, 2, w, w), lambda b, p: (p, 0, 0, 0)),
            pl.BlockSpec((1, 4, 2 * w, LANES), lambda b, p: (p, 0, 0, 0)),
            pl.BlockSpec((1, 4, LANES, 2 * w), lambda b, p: (p, 0, 0, 0)),
            pl.BlockSpec((1, 8, LANES), lambda b, p: (p, 0, 0)),
        ],
        out_specs=pl.BlockSpec((2, nc, w), lambda b, p: (p, b, 0)),
        out_shape=jax.ShapeDtypeStruct(u_l.shape, BF16),
        scratch_shapes=[pltpu.VMEM((4, nc, LANES), F32), pltpu.VMEM((4, nc, LANES), F32)],
        compiler_params=_params(("arbitrary", "arbitrary")),
        name="s5_scan",
    )(u_l, toep, win, wout, coef)


def _layer_norm(r, g, b):
    mu = jnp.mean(r, axis=-1, keepdims=True)
    d = r - mu
    var = jnp.mean(d * d, axis=-1, keepdims=True)
    return d * lax.rsqrt(var + LN_EPS) * g + b


def _mid_body(yf_ref, yb_ref, xs_ref, z_ref, s5_ref, u_ref, g_ref, x_ref, dexp_ref, ng_ref, wout_ref, s5d_ref,
              wga_ref, wgb_ref, wo_ref, l1g_ref, l1b_ref, wr_ref, h_ref, lg_ref):
    y = yf_ref[...].astype(F32) + yb_ref[...].astype(F32) + xs_ref[...].astype(F32) * dexp_ref[...]
    z = z_ref[...].astype(F32)
    y = y * (z * _sigmoid(z))
    gw = SSD_INNER // SSD_GROUPS
    parts = []
    for g in range(SSD_GROUPS):
        yg = y[:, gw * g:gw * (g + 1)]
        ms = jnp.mean(yg * yg, axis=-1, keepdims=True)
        parts.append(yg * lax.rsqrt(ms + RMS_EPS))
    yn = (jnp.concatenate(parts, axis=1) * ng_ref[...]).astype(BF16)
    ya = _dot(yn, wout_ref[...])
    v = s5_ref[...].astype(F32) + u_ref[...].astype(F32) * s5d_ref[...]
    v = jax.nn.gelu(v).astype(BF16)
    yb = _dot(v, wga_ref[...]) * _sigmoid(_dot(v, wgb_ref[...]))
    gt = g_ref[...].astype(F32)
    mix = _sigmoid(gt[:, :D_MODEL]) * ya + _sigmoid(gt[:, D_MODEL:]) * yb
    r = DEEPNORM_ALPHA * x_ref[...] + _dot(mix.astype(BF16), wo_ref[...])
    h = _layer_norm(r, l1g_ref[...], l1b_ref[...])
    h_ref[...] = h
    hh = h.astype(BF16)
    hl = (h - hh.astype(F32)).astype(BF16)
    d = _dot(hh, wr_ref[...])
    lg_ref[...] = d[:, :LANES] + d[:, LANES:] + _dot(hl, wr_ref[:, :LANES])


def _mid(yf, yb, xs, z, s5y, u, gates, x2, dexp, ng, wout, s5d, wga, wgb, wo, l1g, l1b, wr, tb):
    n = x2.shape[0]
    row = lambda a: pl.BlockSpec((tb, a.shape[1]), lambda i: (i, 0))
    full = lambda a: pl.BlockSpec(a.shape, lambda i: (0, 0))
    acts = [yf, yb, xs, z, s5y, u, gates, x2]
    wts = [dexp, ng, wout, s5d, wga, wgb, wo, l1g, l1b, wr]
    return pl.pallas_call(
        _mid_body,
        grid=(n // tb,),
        in_specs=[row(a) for a in acts] + [full(w) for w in wts],
        out_specs=[pl.BlockSpec((tb, D_MODEL), lambda i: (i, 0)), pl.BlockSpec((tb, LANES), lambda i: (i, 0))],
        out_shape=[jax.ShapeDtypeStruct((n, D_MODEL), F32), jax.ShapeDtypeStruct((n, LANES), F32)],
        compiler_params=_params(("arbitrary",)),
        name="merge_ln1",
    )(*acts, *wts)


def _topk_body(lg_ref, gsel_ref, pos_ref, base_ref, aff_ref, *, seq, cap, tb):
    lane = lax.broadcasted_iota(I32, (seq, LANES), 1)
    valid = lane < N_EXPERTS
    lg = jnp.where(valid, lg_ref[...], -jnp.inf)
    m = jnp.max(lg, axis=1, keepdims=True)
    e = jnp.exp(lg - m)
    aff_ref[...] = e / jnp.sum(e, axis=1, keepdims=True)

    def count_ge(t_bits):
        t = pltpu.bitcast(t_bits, F32)
        return jnp.sum((aff_ref[...] >= t).astype(I32), axis=0, keepdims=True)

    def bs(_, c):
        lo, hi = c
        mid = lo + ((hi - lo) >> 1)
        ge = count_ge(mid) >= cap
        return jnp.where(ge, mid, lo), jnp.where(ge, hi, mid)

    lo0 = jnp.zeros((1, LANES), I32)
    hi0 = jnp.full((1, LANES), 0x3F800001, I32)
    thr_bits, _ = lax.fori_loop(0, 31, bs, (lo0, hi0))
    thr = pltpu.bitcast(thr_bits, F32)
    nxt = pltpu.bitcast(thr_bits + 1, F32)
    n_gt = count_ge(thr_bits + 1)
    need = (cap - n_gt).astype(F32)
    ri = lax.broadcasted_iota(I32, (tb, tb), 0)
    ci = lax.broadcasted_iota(I32, (tb, tb), 1)
    tri = (ci < ri).astype(BF16)
    vrow = lax.broadcasted_iota(I32, (1, LANES), 1) < N_EXPERTS

    def blk(j, c):
        ceq, csel = c
        rows = pl.ds(pl.multiple_of(j * tb, tb), tb)
        a = aff_ref[rows, :]
        gt = a >= nxt
        eq = (a >= thr) & jnp.logical_not(gt)
        eqc = ceq + _dot(tri, eq.astype(BF16))
        sel = (gt | (eq & (eqc < need))) & vrow
        self = sel.astype(F32)
        pos = csel + _dot(tri, self.astype(BF16))
        pos_ref[rows, :] = jnp.where(sel, pos, -1.0).astype(I32)
        gsel_ref[rows, :] = jnp.where(sel, a, 0.0)
        base_ref[0, pl.ds(j, 1), :] = csel.astype(I32)
        return (ceq + jnp.sum(eq.astype(F32), axis=0, keepdims=True),
                csel + jnp.sum(self, axis=0, keepdims=True))

    z = jnp.zeros((1, LANES), F32)
    lax.fori_loop(0, seq // tb, blk, (z, z))


def _topk(lg, batch, seq, cap, tb):
    nb = seq // tb
    return pl.pallas_call(
        functools.partial(_topk_body, seq=seq, cap=cap, tb=tb),
        grid=(batch,),
        in_specs=[pl.BlockSpec((seq, LANES), lambda b: (b, 0))],
        out_specs=[
            pl.BlockSpec((seq, LANES), lambda b: (b, 0)),
            pl.BlockSpec((seq, LANES), lambda b: (b, 0)),
            pl.BlockSpec((1, nb, LANES), lambda b: (b, 0, 0)),
        ],
        out_shape=[
            jax.ShapeDtypeStruct((batch * seq, LANES), F32),
            jax.ShapeDtypeStruct((batch * seq, LANES), I32),
            jax.ShapeDtypeStruct((batch, nb, LANES), I32),
        ],
        scratch_shapes=[pltpu.VMEM((seq, LANES), F32)],
        compiler_params=_params(("arbitrary",)),
        name="topk_select",
    )(lg)


def _compact_body(base_ref, pos_ref, idx_ref, *, nb, tb):
    b = pl.program_id(0)
    idx_ref[...] = jnp.zeros_like(idx_ref)
    lane = lax.broadcasted_iota(I32, (tb, LANES), 1)
    trow = lax.broadcasted_iota(I32, (tb, LANES), 0)

    def blk(j, carry):
        rows = pl.ds(pl.multiple_of(j * tb, tb), tb)
        p = pos_ref[rows, :]
        tok = trow + j * tb
        for e in range(N_EXPERTS):
            wb = base_ref[(b * nb + j) * N_EXPERTS + e] // LANES
            rel = p[:, e:e + 1] - wb * LANES
            lo = jnp.sum(jnp.where(rel == lane, tok, 0), axis=0, keepdims=True)
            hi = jnp.sum(jnp.where(rel - LANES == lane, tok, 0), axis=0, keepdims=True)
            idx_ref[0, e, pl.ds(wb, 1), :] += lo
            idx_ref[0, e, pl.ds(wb + 1, 1), :] += hi
        return carry

    lax.fori_loop(0, nb, blk, 0)


def _compact(base_flat, pos, batch, seq, cap, tb):
    nb = seq // tb
    rows = cap // LANES + 8
    return pl.pallas_call(
        functools.partial(_compact_body, nb=nb, tb=tb),
        grid_spec=pltpu.PrefetchScalarGridSpec(
            num_scalar_prefetch=1,
            grid=(batch,),
            in_specs=[pl.BlockSpec((seq, LANES), lambda b, base: (b, 0))],
            out_specs=pl.BlockSpec((1, N_EXPERTS, rows, LANES), lambda b, base: (b, 0, 0, 0)),
        ),
        out_shape=jax.ShapeDtypeStruct((batch, N_EXPERTS, rows, LANES), I32),
        compiler_params=_params(("arbitrary",)),
        name="slot_compact",
    )(base_flat, pos)


def _ffn_body(*refs, nf, nbatch, seq, cap):
    idx_refs = refs[:nbatch]
    idxn_refs = refs[nbatch:2 * nbatch]
    h3_ref, w1_ref, w3_ref, w2_ref, y_ref, xbuf_ref, xb_ref, acc_ref, sem = refs[2 * nbatch:]
    e = pl.program_id(0)
    f = pl.program_id(1)
    ne = pl.num_programs(0)
    slot = e % 2
    nk = D_MODEL // LANES

    def issue(irefs, sl):
        for bi in range(nbatch):
            def body(s, c, bi=bi):
                t = irefs[bi][0, 0, s] + bi * seq
                pltpu.make_async_copy(h3_ref.at[t], xbuf_ref.at[sl, :, bi * cap + s, :], sem.at[sl]).start()
                return c

            lax.fori_loop(0, cap, body, 0, unroll=8)

    @pl.when((f == 0) & (e == 0))
    def _():
        issue(idx_refs, slot)

    @pl.when(f == 0)
    def _():
        pltpu.make_async_copy(xbuf_ref.at[slot], xbuf_ref.at[slot], sem.at[slot]).wait()
        for k in range(nk):
            xb_ref[:, LANES * k:LANES * (k + 1)] = xbuf_ref[slot, k].astype(BF16)
        acc_ref[...] = jnp.zeros_like(acc_ref)

    @pl.when((f == 1) & (e + 1 < ne))
    def _():
        issue(idxn_refs, 1 - slot)

    w1 = w1_ref[0].astype(BF16)
    w3 = w3_ref[0].astype(BF16)
    w2 = w2_ref[0].astype(BF16)
    for bi in range(nbatch):
        rows = slice(bi * cap, (bi + 1) * cap)
        x = xb_ref[rows, :]
        a = _dot(x, w1)
        g = _dot(x, w3)
        hid = (a * _sigmoid(a) * g).astype(BF16)
        acc_ref[rows, :] += _dot(hid, w2)

    @pl.when(f == nf - 1)
    def _():
        for bi in range(nbatch):
            y_ref[bi, 0] = acc_ref[bi * cap:(bi + 1) * cap, :].astype(y_ref.dtype)


def _ffn(idx3, h3, w1, w3, w2, batch, seq, cap, ft):
    ne, d, ff = w1.shape
    nf = ff // ft
    assert nf >= 2
    smem = lambda imap: pl.BlockSpec((1, 1, cap), imap, memory_space=pltpu.SMEM)
    cur = [smem(lambda e, f, bi=bi: (bi * ne + e, 0, 0)) for bi in range(batch)]
    nxt = [smem(lambda e, f, bi=bi: (bi * ne + jnp.minimum(e + 1, ne - 1), 0, 0)) for bi in range(batch)]
    return pl.pallas_call(
        functools.partial(_ffn_body, nf=nf, nbatch=batch, seq=seq, cap=cap),
        grid=(ne, nf),
        in_specs=cur + nxt + [
            pl.BlockSpec(memory_space=pl.ANY),
            pl.BlockSpec((1, d, ft), lambda e, f: (e, 0, f)),
            pl.BlockSpec((1, d, ft), lambda e, f: (e, 0, f)),
            pl.BlockSpec((1, ft, d), lambda e, f: (e, f, 0)),
        ],
        out_specs=pl.BlockSpec((batch, 1, cap, d), lambda e, f: (0, e, 0, 0)),
        out_shape=jax.ShapeDtypeStruct((batch, ne, cap, d), BF16),
        scratch_shapes=[
            pltpu.VMEM((2, d // LANES, batch * cap, LANES), F32),
            pltpu.VMEM((batch * cap, d), BF16),
            pltpu.VMEM((batch * cap, d), F32),
            pltpu.SemaphoreType.DMA((2,)),
        ],
        compiler_params=_params(("arbitrary", "arbitrary")),
        name="expert_ffn",
    )(*([idx3] * (2 * batch)), h3, w1, w3, w2)


def _combine_body(base_ref, pos_ref, gsel_ref, h_ref, y_ref, g_ref, b_ref, o_ref, win_ref, sem,
                  *, nb, tb, cap, win, wsm):
    bi = pl.program_id(0)
    j = pl.program_id(1)
    step = bi * nb + j
    nsteps = pl.num_programs(0) * nb
    slot = step % 2

    def w0_of(s, e, w):
        base = base_ref[s * N_EXPERTS + e]
        return pl.multiple_of(jnp.minimum((base // 16) * 16, cap - w), 16)

    def fits_small(s):
        ok = None
        for e in range(N_EXPERTS):
            nxt = base_ref[jnp.minimum(s + 1, nsteps - 1) * N_EXPERTS + e]
            end = jnp.where(s % nb == nb - 1, cap, nxt)
            ok_e = end <= w0_of(s, e, wsm) + wsm
            ok = ok_e if ok is None else ok & ok_e
        return ok

    def copies(s, sl, w):
        b_s = s // nb
        return [pltpu.make_async_copy(y_ref.at[b_s, e, pl.ds(w0_of(s, e, w), w), :],
                                      win_ref.at[sl, e, pl.ds(0, w), :], sem.at[sl]) for e in range(N_EXPERTS)]

    def start(s, sl):
        small = fits_small(s)

        @pl.when(small)
        def _():
            for cp in copies(s, sl, wsm):
                cp.start()

        @pl.when(jnp.logical_not(small))
        def _():
            for cp in copies(s, sl, win):
                cp.start()

    @pl.when(step == 0)
    def _():
        start(step, slot)

    @pl.when(step + 1 < nsteps)
    def _():
        start(step + 1, 1 - slot)

    def finish(w):
        for cp in copies(step, slot, w):
            cp.wait()
        p = pos_ref[...]
        gs = gsel_ref[...]
        lane = lax.broadcasted_iota(I32, (tb, w), 1)
        acc = jnp.zeros((tb, D_MODEL), F32)
        for e in range(N_EXPERTS):
            rel = p[:, e:e + 1] - w0_of(step, e, w)
            s_e = jnp.where(rel == lane, gs[:, e:e + 1], 0.0).astype(BF16)
            acc = acc + _dot(s_e, win_ref[slot, e, 0:w, :])
        r = DEEPNORM_ALPHA * h_ref[...] + acc
        o_ref[...] = _layer_norm(r, g_ref[...], b_ref[...])

    small_now = fits_small(step)

    @pl.when(small_now)
    def _():
        finish(wsm)

    @pl.when(jnp.logical_not(small_now))
    def _():
        finish(win)


def _combine(base_flat, pos, gsel, h, y, g, b, batch, seq, cap, tb, wsm=48):
    nb = seq // tb
    win = tb + 16
    return pl.pallas_call(
        functools.partial(_combine_body, nb=nb, tb=tb, cap=cap, win=win, wsm=wsm),
        grid_spec=pltpu.PrefetchScalarGridSpec(
            num_scalar_prefetch=1,
            grid=(batch, nb),
            in_specs=[
                pl.BlockSpec((tb, LANES), lambda bi, j, base: (bi * nb + j, 0)),
                pl.BlockSpec((tb, LANES), lambda bi, j, base: (bi * nb + j, 0)),
                pl.BlockSpec((tb, D_MODEL), lambda bi, j, base: (bi * nb + j, 0)),
                pl.BlockSpec(memory_space=pl.ANY),
                pl.BlockSpec((1, D_MODEL), lambda bi, j, base: (0, 0)),
                pl.BlockSpec((1, D_MODEL), lambda bi, j, base: (0, 0)),
            ],
            out_specs=pl.BlockSpec((tb, D_MODEL), lambda bi, j, base: (bi * nb + j, 0)),
            scratch_shapes=[pltpu.VMEM((2, N_EXPERTS, win, D_MODEL), BF16), pltpu.SemaphoreType.DMA((2,))],
        ),
        out_shape=jax.ShapeDtypeStruct((batch * seq, D_MODEL), F32),
        compiler_params=_params(("arbitrary", "arbitrary")),
        name="combine_ln2",
    )(base_flat, pos, gsel, h, y, g, b)


def _pad_lanes(v, width=LANES):
    return jnp.pad(v, [(0, 0)] * (v.ndim - 1) + [(0, width - v.shape[-1])])


def _hi_lo(w):
    hi = w.astype(BF16)
    lo = (w - hi.astype(F32)).astype(BF16)
    return jnp.concatenate([hi, lo], axis=1)


def _layer(x, w_in, conv_w, conv_b, ssd_a_log_f, ssd_a_log_b, ssd_dt_bias_f, ssd_dt_bias_b, ssd_d, ssd_norm_g,
           w_out_ssd, s5_a_re_f, s5_a_im_f, s5_log_step_f, s5_a_re_b, s5_a_im_b, s5_log_step_b, s5_b_re, s5_b_im,
           s5_c_re, s5_c_im, s5_d, w_glu_a, w_glu_b, w_o, ln1_g, ln1_b, w_router, w_e1, w_e3, w_e2, ln2_g, ln2_b):
    batch, seq, d = x.shape
    n = batch * seq
    x2 = x.reshape(n, d)
    c0, c1, c2, c3 = SSD_INNER, SSD_INNER + XBC_DIM, SSD_INNER + XBC_DIM + 2 * SSD_HEADS, \
        SSD_INNER + XBC_DIM + 2 * SSD_HEADS + S5_WIDTH
    wdt = w_in[:, c1:c2]
    wdt = jnp.concatenate([_pad_lanes(wdt[:, :SSD_HEADS]), _pad_lanes(wdt[:, SSD_HEADS:])], axis=1)
    z, xbc, dt, u, gates = _in_proj(x2, w_in[:, :c0].astype(BF16), w_in[:, c0:c1].astype(BF16), _hi_lo(wdt),
                                    w_in[:, c2:c3].astype(BF16), w_in[:, c3:].astype(BF16), tb=512)

    t = S5_CHUNK
    nc = seq // t
    nr = nc // S5_SEGS
    u_l = u.astype(BF16).reshape(batch, S5_SEGS, nr, t, S5_GROUPS, S5_GROUP)
    u_l = u_l.transpose(4, 0, 2, 1, 3, 5).reshape(S5_GROUPS, batch * nc, t * S5_GROUP)
    mf = _s5_mats(s5_a_re_f, s5_a_im_f, s5_log_step_f, s5_b_re, s5_b_im, s5_c_re, s5_c_im, False)
    mb = _s5_mats(s5_a_re_b, s5_a_im_b, s5_log_step_b, s5_b_re, s5_b_im, s5_c_re, s5_c_im, True)
    npair = S5_GROUPS // 2
    toep = (mf[0] + mb[0]).astype(BF16).reshape(npair, 2, t * S5_GROUP, t * S5_GROUP)
    win = jnp.stack([_pair_in(mf[1]), _pair_in(mf[2]), _pair_in(mb[1]), _pair_in(mb[2])], axis=1).astype(BF16)
    wout = jnp.stack([_pair_in(m.transpose(0, 2, 1)).transpose(0, 2, 1) for m in (mf[3], mf[4], mb[3], mb[4])],
                     axis=1).astype(BF16)

    def powc(lr, li, k):
        mag = jnp.exp(lr * k)
        return (mag * jnp.cos(li * k)).reshape(npair, LANES), (mag * jnp.sin(li * k)).reshape(npair, LANES)

    coef = jnp.stack([*powc(*mf[5], float(t)), *powc(*mf[5], float(t * nr)),
                      *powc(*mb[5], float(t)), *powc(*mb[5], float(t * nr))], axis=1)
    s5y = _s5(u_l, toep, win, wout, coef, batch, nc)
    s5y = s5y.reshape(S5_GROUPS, batch, nr, S5_SEGS, t, S5_GROUP).transpose(1, 3, 2, 4, 0, 5).reshape(n, S5_WIDTH)

    xs, bm, cm = _conv(xbc, conv_w, conv_b.reshape(1, XBC_DIM), seq, tb=512)

    def ssd_par(a_log, bias):
        p = jnp.zeros((8, LANES), F32)
        return p.at[0, :SSD_HEADS].set(-jnp.exp(a_log)).at[1, :SSD_HEADS].set(bias)

    yb = _ssd(xs, bm, cm, dt, ssd_par(ssd_a_log_b, ssd_dt_bias_b), batch, seq, True)
    yf = _ssd(xs, bm, cm, dt, ssd_par(ssd_a_log_f, ssd_dt_bias_f), batch, seq, False)

    wr = _hi_lo(_pad_lanes(w_router))
    h, lg = _mid(yf, yb, xs, z, s5y, u, gates, x2, jnp.repeat(ssd_d, SSD_HEADDIM).reshape(1, SSD_INNER),
                 ssd_norm_g.reshape(1, SSD_INNER), w_out_ssd.astype(BF16), s5_d.reshape(1, S5_WIDTH),
                 w_glu_a.astype(BF16), w_glu_b.astype(BF16), w_o.astype(BF16), ln1_g.reshape(1, d),
                 ln1_b.reshape(1, d), wr, tb=256)

    cap = CAPACITY_FACTOR * seq // N_EXPERTS
    tbk = 128
    gsel, pos, base = _topk(lg, batch, seq, cap, tbk)
    base_flat = base[:, :, :N_EXPERTS].reshape(-1)
    idx = _compact(base_flat, pos, batch, seq, cap, tbk)
    idx3 = idx[:, :, :cap // LANES, :].reshape(batch * N_EXPERTS, 1, cap)
    y = _ffn(idx3, h.reshape(n, d // LANES, LANES), w_e1, w_e3, w_e2, batch, seq, cap, ft=256)
    out = _combine(base_flat, pos, gsel, h, y, ln2_g.reshape(1, d), ln2_b.reshape(1, d), batch, seq, cap, tbk)
    return out.reshape(batch, seq, d)


def kernel(x, w_in, conv_w, conv_b, ssd_a_log_f, ssd_a_log_b, ssd_dt_bias_f, ssd_dt_bias_b, ssd_d, ssd_norm_g, w_out_ssd, s5_a_re_f, s5_a_im_f, s5_log_step_f, s5_a_re_b, s5_a_im_b, s5_log_step_b, s5_b_re, s5_b_im, s5_c_re, s5_c_im, s5_d, w_glu_a, w_glu_b, w_o, ln1_g, ln1_b, w_router, w_e1, w_e3, w_e2, ln2_g, ln2_b):
    h = x
    for i in range(DEPTH):
        h = _layer(h, w_in[i], conv_w[i], conv_b[i], ssd_a_log_f[i], ssd_a_log_b[i], ssd_dt_bias_f[i],
                   ssd_dt_bias_b[i], ssd_d[i], ssd_norm_g[i], w_out_ssd[i], s5_a_re_f[i], s5_a_im_f[i],
                   s5_log_step_f[i], s5_a_re_b[i], s5_a_im_b[i], s5_log_step_b[i], s5_b_re[i], s5_b_im[i],
                   s5_c_re[i], s5_c_im[i], s5_d[i], w_glu_a[i], w_glu_b[i], w_o[i], ln1_g[i], ln1_b[i],
                   w_router[i], w_e1[i], w_e3[i], w_e2[i], ln2_g[i], ln2_b[i])
    return h
```

```python
import functools
import math

import jax
import jax.numpy as jnp
import numpy as np
from jax import lax
from jax.experimental import pallas as pl
from jax.experimental.pallas import tpu as pltpu

F32 = jnp.float32
BF16 = jnp.bfloat16
I32 = jnp.int32

D_MODEL = 1024
SSD_HEADDIM = 64
SSD_HEADS = 24
SSD_INNER = SSD_HEADS * SSD_HEADDIM
SSD_GROUPS = 4
SSD_STATE = 128
SSD_CONV = 5
SSD_BC = 2 * SSD_GROUPS * SSD_STATE
XBC_DIM = SSD_INNER + SSD_BC
S5_GROUP = 16
S5_WIDTH = 768
S5_GROUPS = S5_WIDTH // S5_GROUP
S5_STATE = 64
N_EXPERTS = 16
EXPERT_FF = 2816
CAPACITY_FACTOR = 2
DEPTH = 1
DEEPNORM_ALPHA = (2.0 * DEPTH) ** 0.25
LN_EPS = 1e-5
RMS_EPS = 1e-5
LOG2E = 1.4426950408889634

LANES = 128
S5_CHUNK = 8
S5_SEGS = 8
SSD_Q = 128
CONV_HALO = 16
VMEM_LIMIT = 56 * 1024 * 1024


def _dot(a, b):
    return jnp.dot(a, b, preferred_element_type=F32)


def _split3(v):
    v1 = v.astype(BF16)
    r1 = v - v1.astype(F32)
    v2 = r1.astype(BF16)
    v3 = (r1 - v2.astype(F32)).astype(BF16)
    return v1, v2, v3


def _dot_exact_lhs(m_bf16, v):
    v1, v2, v3 = _split3(v)
    return _dot(m_bf16, v1) + _dot(m_bf16, v2) + _dot(m_bf16, v3)


def _sigmoid(x):
    return 1.0 / (1.0 + jnp.exp(-x))


def _params(sem, **kw):
    return pltpu.CompilerParams(dimension_semantics=sem, vmem_limit_bytes=VMEM_LIMIT, **kw)


def _in_proj_body(x_ref, wz_ref, wxbc_ref, wdt_ref, wu_ref, wg_ref, z_ref, xbc_ref, dt_ref, u_ref, g_ref):
    x = x_ref[...]
    xh = x.astype(BF16)
    xl = (x - xh.astype(F32)).astype(BF16)
    z_ref[...] = _dot(xh, wz_ref[...]).astype(z_ref.dtype)
    xbc_ref[...] = _dot(xh, wxbc_ref[...]).astype(xbc_ref.dtype)
    u_ref[...] = _dot(xh, wu_ref[...]).astype(u_ref.dtype)
    g_ref[...] = _dot(xh, wg_ref[...]).astype(g_ref.dtype)
    d = _dot(xh, wdt_ref[...])
    nd = dt_ref.shape[-1]
    dt_ref[...] = d[:, :nd] + d[:, nd:] + _dot(xl, wdt_ref[:, :nd])


def _in_proj(x2, wz, wxbc, wdt, wu, wg, tb):
    n = x2.shape[0]
    full = lambda w: pl.BlockSpec(w.shape, lambda i: (0, 0))
    row = lambda c: pl.BlockSpec((tb, c), lambda i: (i, 0))
    nd = wdt.shape[1] // 2
    return pl.pallas_call(
        _in_proj_body,
        grid=(n // tb,),
        in_specs=[row(D_MODEL), full(wz), full(wxbc), full(wdt), full(wu), full(wg)],
        out_specs=[row(SSD_INNER), row(XBC_DIM), row(nd), row(S5_WIDTH), row(2 * D_MODEL)],
        out_shape=[
            jax.ShapeDtypeStruct((n, SSD_INNER), BF16),
            jax.ShapeDtypeStruct((n, XBC_DIM), BF16),
            jax.ShapeDtypeStruct((n, nd), F32),
            jax.ShapeDtypeStruct((n, S5_WIDTH), BF16),
            jax.ShapeDtypeStruct((n, 2 * D_MODEL), BF16),
        ],
        compiler_params=_params(("arbitrary",)),
        name="in_proj",
    )(x2, wz, wxbc, wdt, wu, wg)


def _conv_body(cur_ref, prev_ref, next_ref, w_ref, b_ref, xs_ref, bm_ref, cm_ref, ext_ref, *, tb, blocks_per_seq):
    i = pl.program_id(0)
    first = (i % blocks_per_seq) == 0
    last = (i % blocks_per_seq) == blocks_per_seq - 1
    hr = CONV_HALO
    ext_ref[0:hr, :] = jnp.where(first, 0.0, prev_ref[0].astype(F32))
    ext_ref[hr:hr + tb, :] = cur_ref[...].astype(F32)
    ext_ref[hr + tb:2 * hr + tb, :] = jnp.where(last, 0.0, next_ref[0].astype(F32))
    half = SSD_CONV // 2
    cw = 512
    for c0 in range(0, XBC_DIM, cw):
        acc = jnp.broadcast_to(b_ref[:, c0:c0 + cw], (tb, cw))
        for j in range(SSD_CONV):
            acc = acc + ext_ref[hr - half + j:hr - half + j + tb, c0:c0 + cw] * w_ref[j:j + 1, c0:c0 + cw]
        y = (acc * _sigmoid(acc)).astype(BF16)
        if c0 + cw <= SSD_INNER:
            xs_ref[:, c0:c0 + cw] = y
        elif c0 < SSD_INNER + SSD_BC // 2:
            bm_ref[:, c0 - SSD_INNER:c0 - SSD_INNER + cw] = y
        else:
            o = c0 - SSD_INNER - SSD_BC // 2
            cm_ref[:, o:o + cw] = y


def _conv(xbc, conv_w, conv_b, seq, tb):
    n = xbc.shape[0]
    nb = n // tb
    hr = CONV_HALO
    xbc3 = xbc.reshape(n // hr, hr, XBC_DIM)
    r = tb // hr
    return pl.pallas_call(
        functools.partial(_conv_body, tb=tb, blocks_per_seq=seq // tb),
        grid=(nb,),
        in_specs=[
            pl.BlockSpec((tb, XBC_DIM), lambda i: (i, 0)),
            pl.BlockSpec((1, hr, XBC_DIM), lambda i: (jnp.maximum(i * r - 1, 0), 0, 0)),
            pl.BlockSpec((1, hr, XBC_DIM), lambda i: (jnp.minimum((i + 1) * r, n // hr - 1), 0, 0)),
            pl.BlockSpec((SSD_CONV, XBC_DIM), lambda i: (0, 0)),
            pl.BlockSpec((1, XBC_DIM), lambda i: (0, 0)),
        ],
        out_specs=[
            pl.BlockSpec((tb, SSD_INNER), lambda i: (i, 0)),
            pl.BlockSpec((tb, SSD_BC // 2), lambda i: (i, 0)),
            pl.BlockSpec((tb, SSD_BC // 2), lambda i: (i, 0)),
        ],
        out_shape=[
            jax.ShapeDtypeStruct((n, SSD_INNER), BF16),
            jax.ShapeDtypeStruct((n, SSD_BC // 2), BF16),
            jax.ShapeDtypeStruct((n, SSD_BC // 2), BF16),
        ],
        scratch_shapes=[pltpu.VMEM((tb + 2 * hr, XBC_DIM), F32)],
        compiler_params=_params(("arbitrary",)),
        name="conv_silu",
    )(xbc, xbc3, xbc3, conv_w, conv_b)


def _ssd_body(xs_ref, bm_ref, cm_ref, dt_ref, par_ref, y_ref, st_ref, *, reverse, q):
    c = pl.program_id(1)

    @pl.when(c == 0)
    def _():
        st_ref[...] = jnp.zeros_like(st_ref)

    a = par_ref[0:1, :]
    bias = par_ref[1:2, :]
    dt = jax.nn.softplus(dt_ref[...] + bias)
    dta = dt * a
    ri = lax.broadcasted_iota(I32, (q, q), 0)
    ci = lax.broadcasted_iota(I32, (q, q), 1)
    mask = (ci >= ri) if reverse else (ci <= ri)
    cum = _dot_exact_lhs(mask.astype(BF16), dta)
    total = cum[0:1, :] if reverse else cum[q - 1:q, :]
    cum2 = cum * LOG2E
    tot2 = total * LOG2E
    rowp = cum2.T - jnp.log2(dt.T)
    ecum = jnp.exp2(cum2)
    wst = dt * jnp.exp2(tot2 - cum2)
    etot = jnp.exp2(tot2)
    lane = lax.broadcasted_iota(I32, (q, LANES), 1)
    lo = lane < SSD_HEADDIM
    hpg = SSD_HEADS // SSD_GROUPS
    gw = hpg * SSD_HEADDIM
    lane_g = lax.broadcasted_iota(I32, (1, gw), 1) // SSD_HEADDIM
    for g in range(SSD_GROUPS):
        bg = bm_ref[:, SSD_STATE * g:SSD_STATE * (g + 1)]
        cg = cm_ref[:, SSD_STATE * g:SSD_STATE * (g + 1)]
        cb = lax.dot_general(cg, bg, (((1,), (1,)), ((), ())), preferred_element_type=F32)
        st = st_ref[g]
        yoff = _dot(cg, st.astype(BF16))
        bgt = bg.astype(F32).T.astype(BF16)
        xw_parts = []
        for jp in range(hpg // 2):
            h0 = hpg * g + 2 * jp
            xt = xs_ref[:, SSD_HEADDIM * h0:SSD_HEADDIM * h0 + LANES]
            ws = []
            for hh in (h0, h0 + 1):
                seg = cum2[:, hh:hh + 1] - rowp[hh:hh + 1, :]
                e = jnp.exp2(jnp.where(mask, seg, -jnp.inf))
                ws.append((cb * e).astype(BF16))
            wp = jnp.concatenate(ws, axis=1)
            zero = jnp.zeros_like(xt)
            rhs = jnp.concatenate([jnp.where(lo, xt, zero), jnp.where(lo, zero, xt)], axis=0)
            yd = _dot(wp, rhs)
            ec = jnp.where(lo, ecum[:, h0:h0 + 1], ecum[:, h0 + 1:h0 + 2])
            y_ref[:, SSD_HEADDIM * h0:SSD_HEADDIM * h0 + LANES] = (
                yd + yoff[:, LANES * jp:LANES * (jp + 1)] * ec).astype(y_ref.dtype)
            wc = jnp.where(lo, wst[:, h0:h0 + 1], wst[:, h0 + 1:h0 + 2])
            xw_parts.append((xt.astype(F32) * wc).astype(BF16))
        xw = jnp.concatenate(xw_parts, axis=1)
        snew = _dot(bgt, xw)
        eg = jnp.zeros((1, gw), F32)
        for j in range(hpg):
            eg = jnp.where(lane_g == j, etot[:, hpg * g + j:hpg * g + j + 1], eg)
        st_ref[g] = st * eg + snew


def _ssd(xs, bm, cm, dt, par, batch, seq, reverse):
    n = xs.shape[0]
    q = SSD_Q
    nc = seq // q
    d = 1 if reverse else 0

    def blk(b, c):
        return b * nc + (nc - 1 - c if reverse else c)

    gw = (SSD_HEADS // SSD_GROUPS) * SSD_HEADDIM
    return pl.pallas_call(
        functools.partial(_ssd_body, reverse=reverse, q=q),
        grid=(batch, nc),
        in_specs=[
            pl.BlockSpec((q, SSD_INNER), lambda b, c: (blk(b, c), 0)),
            pl.BlockSpec((q, SSD_BC // 2), lambda b, c: (blk(b, c), 0)),
            pl.BlockSpec((q, SSD_BC // 2), lambda b, c: (blk(b, c), 0)),
            pl.BlockSpec((q, LANES), lambda b, c: (blk(b, c), d)),
            pl.BlockSpec((8, LANES), lambda b, c: (0, 0)),
        ],
        out_specs=pl.BlockSpec((q, SSD_INNER), lambda b, c: (blk(b, c), 0)),
        out_shape=jax.ShapeDtypeStruct((n, SSD_INNER), BF16),
        scratch_shapes=[pltpu.VMEM((SSD_GROUPS, SSD_STATE, gw), F32)],
        compiler_params=_params(("arbitrary", "arbitrary")),
        name="ssd_bwd" if reverse else "ssd_fwd",
    )(xs, bm, cm, dt, par)


def _s5_mats(a_re, a_im, log_step, b_re, b_im, c_re, c_im, reverse):
    t = S5_CHUNK
    hp = lax.Precision.HIGHEST
    step = jnp.exp(log_step.astype(F32))[:, None]
    lr = a_re.astype(F32) * step
    li = a_im.astype(F32) * step
    pw = jnp.arange(t + 1, dtype=F32)[None, :, None]
    mag = jnp.exp(lr[:, None, :] * pw)
    pr = mag * jnp.cos(li[:, None, :] * pw)
    pi = mag * jnp.sin(li[:, None, :] * pw)
    ar, ai = pr[:, 1], pi[:, 1]
    den = a_re.astype(F32) ** 2 + a_im.astype(F32) ** 2
    qr = ((ar - 1.0) * a_re + ai * a_im) / den
    qi = (ai * a_re - (ar - 1.0) * a_im) / den
    br = qr[..., None] * b_re - qi[..., None] * b_im
    bi = qr[..., None] * b_im + qi[..., None] * b_re
    cpr = c_re[:, None] * pr[:, :t, None, :] - c_im[:, None] * pi[:, :t, None, :]
    cpi = c_re[:, None] * pi[:, :t, None, :] + c_im[:, None] * pr[:, :t, None, :]
    m = jnp.einsum("gtkp,gpj->gtkj", cpr, br, precision=hp) - jnp.einsum("gtkp,gpj->gtkj", cpi, bi, precision=hp)
    s_i = np.arange(t)[:, None, None]
    t_i = np.arange(t)[None, :, None]
    lag = np.arange(t)[None, None, :]
    pick = ((s_i - t_i) if reverse else (t_i - s_i)) == lag
    mt = jnp.einsum("stu,gukj->gsjtk", jnp.asarray(pick, F32), m, precision=hp)
    toep = mt.reshape(S5_GROUPS, t * S5_GROUP, t * S5_GROUP)
    e_in = (jnp.arange(t) if reverse else (t - 1 - jnp.arange(t)))
    wr = pr[:, e_in][..., None] * br[:, None] - pi[:, e_in][..., None] * bi[:, None]
    wi = pr[:, e_in][..., None] * bi[:, None] + pi[:, e_in][..., None] * br[:, None]
    win_re = wr.transpose(0, 1, 3, 2).reshape(S5_GROUPS, t * S5_GROUP, S5_STATE)
    win_im = wi.transpose(0, 1, 3, 2).reshape(S5_GROUPS, t * S5_GROUP, S5_STATE)
    e_out = (t - jnp.arange(t)) if reverse else (jnp.arange(t) + 1)
    gr = c_re[:, None] * pr[:, e_out, None, :] - c_im[:, None] * pi[:, e_out, None, :]
    gi = c_re[:, None] * pi[:, e_out, None, :] + c_im[:, None] * pr[:, e_out, None, :]
    wout_re = gr.transpose(0, 3, 1, 2).reshape(S5_GROUPS, S5_STATE, t * S5_GROUP)
    wout_im = (-gi).transpose(0, 3, 1, 2).reshape(S5_GROUPS, S5_STATE, t * S5_GROUP)
    return toep, win_re, win_im, wout_re, wout_im, (lr, li)


S5_GPT = LANES // S5_GROUP
S5_TILES = S5_WIDTH // LANES


def _tile_toep(w):
    t = w.shape[1] // S5_GROUP
    w6 = w.reshape(S5_TILES, S5_GPT, t, S5_GROUP, t, S5_GROUP).transpose(0, 2, 1, 3, 4, 5)
    eye = jnp.eye(S5_GPT, dtype=w.dtype)
    out = w6[:, :, :, :, :, None, :] * eye[None, None, :, None, None, :, None]
    return out.reshape(S5_TILES, t * LANES, t * LANES)


def _tile_in(w):
    t = w.shape[1] // S5_GROUP
    w5 = w.reshape(S5_TILES, S5_GPT, t, S5_GROUP, w.shape[2]).transpose(0, 2, 1, 3, 4)
    eye = jnp.eye(S5_GPT, dtype=w.dtype)
    out = w5[:, :, :, :, None, :] * eye[None, None, :, None, :, None]
    return out.reshape(S5_TILES, t * LANES, S5_GPT * w.shape[2])


def _tile_out(w):
    t = w.shape[2] // S5_GROUP
    w5 = w.reshape(S5_TILES, S5_GPT, w.shape[1], t, S5_GROUP)
    eye = jnp.eye(S5_GPT, dtype=w.dtype)
    out = w5[:, :, :, :, None, :] * eye[None, :, None, None, :, None]
    return out.reshape(S5_TILES, S5_GPT * w.shape[1], t * LANES)


def _cmul(xr, xi, ar, ai):
    return xr * ar - xi * ai, xr * ai + xi * ar


def _s5_body(*refs, nt, nseg_rows):
    u_refs = refs[:nt]
    toep_ref, win_ref, wout_ref, co_ref = refs[nt:nt + 4]
    y_refs = refs[nt + 4:2 * nt + 4]
    s_ref = refs[2 * nt + 4]
    x_ref = s_ref
    up = jnp.concatenate([r[...] for r in u_refs], axis=1)
    for k in range(4):
        s_ref[k] = _dot(up, win_ref[0, k])
    co = co_ref[0]
    nr = nseg_rows
    sw = co.shape[-1]
    zero = jnp.zeros((S5_SEGS, sw), F32)

    def scan(kr, ki, ar, ai, asr, asi, reverse):
        def rows(i):
            r = (nr - 1 - i) if reverse else i
            return pl.ds(pl.multiple_of(r * S5_SEGS, S5_SEGS), S5_SEGS)

        def p1(i, c):
            xr, xi = _cmul(c[0], c[1], ar, ai)
            return xr + s_ref[kr, rows(i), :], xi + s_ref[ki, rows(i), :]

        er, ei = lax.fori_loop(0, nr, p1, (zero, zero))
        order = range(S5_SEGS - 1, -1, -1) if reverse else range(S5_SEGS)
        cr = jnp.zeros((1, sw), F32)
        ci = jnp.zeros((1, sw), F32)
        crs, cis = [None] * S5_SEGS, [None] * S5_SEGS
        for j in order:
            crs[j], cis[j] = cr, ci
            nr_, ni_ = _cmul(cr, ci, asr, asi)
            cr, ci = nr_ + er[j:j + 1], ni_ + ei[j:j + 1]
        c0 = (jnp.concatenate(crs, axis=0), jnp.concatenate(cis, axis=0))

        def p2(i, c):
            sr = s_ref[kr, rows(i), :]
            si = s_ref[ki, rows(i), :]
            x_ref[kr, rows(i), :] = c[0]
            x_ref[ki, rows(i), :] = c[1]
            xr, xi = _cmul(c[0], c[1], ar, ai)
            return xr + sr, xi + si

        lax.fori_loop(0, nr, p2, c0)

    scan(0, 1, co[0:1], co[1:2], co[2:3], co[3:4], False)
    scan(2, 3, co[4:5], co[5:6], co[6:7], co[7:8], True)
    y = _dot(up, toep_ref[0])
    for k in range(4):
        y = y + _dot(x_ref[k].astype(BF16), wout_ref[0, k])
    for t in range(nt):
        y_refs[t][...] = y[:, LANES * t:LANES * (t + 1)].astype(y_refs[t].dtype)


def _s5(u2, toep, win, wout, coef, batch, nc):
    nt = u2.shape[1] // S5_WIDTH
    sw = win.shape[-1]
    col = lambda t: pl.BlockSpec((nc, LANES), lambda j, b, t=t: (b, S5_TILES * t + j))
    return pl.pallas_call(
        functools.partial(_s5_body, nt=nt, nseg_rows=nc // S5_SEGS),
        grid=(S5_TILES, batch),
        in_specs=[col(t) for t in range(nt)] + [
            pl.BlockSpec((1, nt * LANES, nt * LANES), lambda j, b: (j, 0, 0)),
            pl.BlockSpec((1, 4, nt * LANES, sw), lambda j, b: (j, 0, 0, 0)),
            pl.BlockSpec((1, 4, sw, nt * LANES), lambda j, b: (j, 0, 0, 0)),
            pl.BlockSpec((1, 8, sw), lambda j, b: (j, 0, 0)),
        ],
        out_specs=[pl.BlockSpec((nc, LANES), lambda j, b: (b, j)) for _ in range(nt)],
        out_shape=[jax.ShapeDtypeStruct((batch * nc, S5_WIDTH), BF16) for _ in range(nt)],
        scratch_shapes=[pltpu.VMEM((4, nc, sw), F32)],
        compiler_params=_params(("arbitrary", "arbitrary")),
        name="s5_scan",
    )(*([u2] * nt), toep, win, wout, coef)


def _layer_norm(r, g, b):
    mu = jnp.mean(r, axis=-1, keepdims=True)
    d = r - mu
    var = jnp.mean(d * d, axis=-1, keepdims=True)
    return d * lax.rsqrt(var + LN_EPS) * g + b


def _mid_body(yf_ref, yb_ref, xs_ref, z_ref, s5_ref, u_ref, g_ref, x_ref, dexp_ref, ng_ref, wout_ref, s5d_ref,
              wga_ref, wgb_ref, wo_ref, l1g_ref, l1b_ref, wr_ref, h_ref, lg_ref):
    y = yf_ref[...].astype(F32) + yb_ref[...].astype(F32) + xs_ref[...].astype(F32) * dexp_ref[...]
    z = z_ref[...].astype(F32)
    y = y * (z * _sigmoid(z))
    gw = SSD_INNER // SSD_GROUPS
    parts = []
    for g in range(SSD_GROUPS):
        yg = y[:, gw * g:gw * (g + 1)]
        ms = jnp.mean(yg * yg, axis=-1, keepdims=True)
        parts.append(yg * lax.rsqrt(ms + RMS_EPS))
    yn = (jnp.concatenate(parts, axis=1) * ng_ref[...]).astype(BF16)
    ya = _dot(yn, wout_ref[...])
    v = s5_ref[...].astype(F32) + u_ref[...].astype(F32) * s5d_ref[...]
    v = jax.nn.gelu(v).astype(BF16)
    yb = _dot(v, wga_ref[...]) * _sigmoid(_dot(v, wgb_ref[...]))
    gt = g_ref[...].astype(F32)
    mix = _sigmoid(gt[:, :D_MODEL]) * ya + _sigmoid(gt[:, D_MODEL:]) * yb
    r = DEEPNORM_ALPHA * x_ref[...] + _dot(mix.astype(BF16), wo_ref[...])
    h = _layer_norm(r, l1g_ref[...], l1b_ref[...])
    h_ref[...] = h
    hh = h.astype(BF16)
    hl = (h - hh.astype(F32)).astype(BF16)
    d = _dot(hh, wr_ref[...])
    lg_ref[...] = d[:, :LANES] + d[:, LANES:] + _dot(hl, wr_ref[:, :LANES])


def _mid(yf, yb, xs, z, s5y, u, gates, x2, dexp, ng, wout, s5d, wga, wgb, wo, l1g, l1b, wr, tb):
    n = x2.shape[0]
    row = lambda a: pl.BlockSpec((tb, a.shape[1]), lambda i: (i, 0))
    full = lambda a: pl.BlockSpec(a.shape, lambda i: (0, 0))
    acts = [yf, yb, xs, z, s5y, u, gates, x2]
    wts = [dexp, ng, wout, s5d, wga, wgb, wo, l1g, l1b, wr]
    return pl.pallas_call(
        _mid_body,
        grid=(n // tb,),
        in_specs=[row(a) for a in acts] + [full(w) for w in wts],
        out_specs=[pl.BlockSpec((tb, D_MODEL), lambda i: (i, 0)), pl.BlockSpec((tb, LANES), lambda i: (i, 0))],
        out_shape=[jax.ShapeDtypeStruct((n, D_MODEL), F32), jax.ShapeDtypeStruct((n, LANES), F32)],
        compiler_params=_params(("arbitrary",)),
        name="merge_ln1",
    )(*acts, *wts)


def _topk_body(lg_ref, gsel_ref, pos_ref, base_ref, aff_ref, *, seq, cap, tb):
    lane = lax.broadcasted_iota(I32, (seq, LANES), 1)
    valid = lane < N_EXPERTS
    lg = jnp.where(valid, lg_ref[...], -jnp.inf)
    m = jnp.max(lg, axis=1, keepdims=True)
    e = jnp.exp(lg - m)
    aff_ref[...] = e / jnp.sum(e, axis=1, keepdims=True)

    def count_ge(t_bits):
        t = pltpu.bitcast(t_bits, F32)
        return jnp.sum((aff_ref[...] >= t).astype(I32), axis=0, keepdims=True)

    def bs(_, c):
        lo, hi = c
        mid = lo + ((hi - lo) >> 1)
        ge = count_ge(mid) >= cap
        return jnp.where(ge, mid, lo), jnp.where(ge, hi, mid)

    lo0 = jnp.zeros((1, LANES), I32)
    hi0 = jnp.full((1, LANES), 0x3F800001, I32)
    thr_bits, _ = lax.fori_loop(0, 31, bs, (lo0, hi0))
    thr = pltpu.bitcast(thr_bits, F32)
    nxt = pltpu.bitcast(thr_bits + 1, F32)
    n_gt = count_ge(thr_bits + 1)
    need = (cap - n_gt).astype(F32)
    ri = lax.broadcasted_iota(I32, (tb, tb), 0)
    ci = lax.broadcasted_iota(I32, (tb, tb), 1)
    tri = (ci < ri).astype(BF16)
    vrow = lax.broadcasted_iota(I32, (1, LANES), 1) < N_EXPERTS

    def blk(j, c):
        ceq, csel = c
        rows = pl.ds(pl.multiple_of(j * tb, tb), tb)
        a = aff_ref[rows, :]
        gt = a >= nxt
        eq = (a >= thr) & jnp.logical_not(gt)
        eqc = ceq + _dot(tri, eq.astype(BF16))
        sel = (gt | (eq & (eqc < need))) & vrow
        self = sel.astype(F32)
        pos = csel + _dot(tri, self.astype(BF16))
        pos_ref[rows, :] = jnp.where(sel, pos, -1.0).astype(I32)
        gsel_ref[rows, :] = jnp.where(sel, a, 0.0)
        base_ref[0, pl.ds(j, 1), :] = csel.astype(I32)
        return (ceq + jnp.sum(eq.astype(F32), axis=0, keepdims=True),
                csel + jnp.sum(self, axis=0, keepdims=True))

    z = jnp.zeros((1, LANES), F32)
    lax.fori_loop(0, seq // tb, blk, (z, z))


def _topk(lg, batch, seq, cap, tb):
    nb = seq // tb
    return pl.pallas_call(
        functools.partial(_topk_body, seq=seq, cap=cap, tb=tb),
        grid=(batch,),
        in_specs=[pl.BlockSpec((seq, LANES), lambda b: (b, 0))],
        out_specs=[
            pl.BlockSpec((seq, LANES), lambda b: (b, 0)),
            pl.BlockSpec((seq, LANES), lambda b: (b, 0)),
            pl.BlockSpec((1, nb, LANES), lambda b: (b, 0, 0)),
        ],
        out_shape=[
            jax.ShapeDtypeStruct((batch * seq, LANES), F32),
            jax.ShapeDtypeStruct((batch * seq, LANES), I32),
            jax.ShapeDtypeStruct((batch, nb, LANES), I32),
        ],
        scratch_shapes=[pltpu.VMEM((seq, LANES), F32)],
        compiler_params=_params(("arbitrary",)),
        name="topk_select",
    )(lg)


def _compact_body(base_ref, pos_ref, idx_ref, *, nb, tb):
    b = pl.program_id(0)
    idx_ref[...] = jnp.zeros_like(idx_ref)
    lane = lax.broadcasted_iota(I32, (tb, LANES), 1)
    trow = lax.broadcasted_iota(I32, (tb, LANES), 0)

    def blk(j, carry):
        rows = pl.ds(pl.multiple_of(j * tb, tb), tb)
        p = pos_ref[rows, :]
        tok = trow + j * tb
        for e in range(N_EXPERTS):
            wb = base_ref[(b * nb + j) * N_EXPERTS + e] // LANES
            rel = p[:, e:e + 1] - wb * LANES
            lo = jnp.sum(jnp.where(rel == lane, tok, 0), axis=0, keepdims=True)
            hi = jnp.sum(jnp.where(rel - LANES == lane, tok, 0), axis=0, keepdims=True)
            idx_ref[0, e, pl.ds(wb, 1), :] += lo
            idx_ref[0, e, pl.ds(wb + 1, 1), :] += hi
        return carry

    lax.fori_loop(0, nb, blk, 0)


def _compact(base_flat, pos, batch, seq, cap, tb):
    nb = seq // tb
    rows = cap // LANES + 8
    return pl.pallas_call(
        functools.partial(_compact_body, nb=nb, tb=tb),
        grid_spec=pltpu.PrefetchScalarGridSpec(
            num_scalar_prefetch=1,
            grid=(batch,),
            in_specs=[pl.BlockSpec((seq, LANES), lambda b, base: (b, 0))],
            out_specs=pl.BlockSpec((1, N_EXPERTS, rows, LANES), lambda b, base: (b, 0, 0, 0)),
        ),
        out_shape=jax.ShapeDtypeStruct((batch, N_EXPERTS, rows, LANES), I32),
        compiler_params=_params(("arbitrary",)),
        name="slot_compact",
    )(base_flat, pos)


def _ffn_body(*refs, nf, nbatch, seq, cap):
    idx_refs = refs[:nbatch]
    idxn_refs = refs[nbatch:2 * nbatch]
    h3_ref, w1_ref, w3_ref, w2_ref, y_ref, xbuf_ref, xb_ref, acc_ref, sem = refs[2 * nbatch:]
    e = pl.program_id(0)
    f = pl.program_id(1)
    ne = pl.num_programs(0)
    slot = e % 2
    nk = D_MODEL // LANES

    def issue(irefs, sl):
        for bi in range(nbatch):
            def body(s, c, bi=bi):
                t = irefs[bi][0, 0, s] + bi * seq
                pltpu.make_async_copy(h3_ref.at[t], xbuf_ref.at[sl, :, bi * cap + s, :], sem.at[sl]).start()
                return c

            lax.fori_loop(0, cap, body, 0, unroll=8)

    @pl.when((f == 0) & (e == 0))
    def _():
        issue(idx_refs, slot)

    @pl.when(f == 0)
    def _():
        pltpu.make_async_copy(xbuf_ref.at[slot], xbuf_ref.at[slot], sem.at[slot]).wait()
        for k in range(nk):
            xb_ref[:, LANES * k:LANES * (k + 1)] = xbuf_ref[slot, k].astype(BF16)
        acc_ref[...] = jnp.zeros_like(acc_ref)

    @pl.when((f == 1) & (e + 1 < ne))
    def _():
        issue(idxn_refs, 1 - slot)

    w1 = w1_ref[0].astype(BF16)
    w3 = w3_ref[0].astype(BF16)
    w2 = w2_ref[0].astype(BF16)
    for bi in range(nbatch):
        rows = slice(bi * cap, (bi + 1) * cap)
        x = xb_ref[rows, :]
        a = _dot(x, w1)
        g = _dot(x, w3)
        hid = (a * _sigmoid(a) * g).astype(BF16)
        acc_ref[rows, :] += _dot(hid, w2)

    @pl.when(f == nf - 1)
    def _():
        for bi in range(nbatch):
            y_ref[bi, 0] = acc_ref[bi * cap:(bi + 1) * cap, :].astype(y_ref.dtype)


def _ffn(idx3, h3, w1, w3, w2, batch, seq, cap, ft):
    ne, d, ff = w1.shape
    nf = ff // ft
    assert nf >= 2
    smem = lambda imap: pl.BlockSpec((1, 1, cap), imap, memory_space=pltpu.SMEM)
    cur = [smem(lambda e, f, bi=bi: (bi * ne + e, 0, 0)) for bi in range(batch)]
    nxt = [smem(lambda e, f, bi=bi: (bi * ne + jnp.minimum(e + 1, ne - 1), 0, 0)) for bi in range(batch)]
    return pl.pallas_call(
        functools.partial(_ffn_body, nf=nf, nbatch=batch, seq=seq, cap=cap),
        grid=(ne, nf),
        in_specs=cur + nxt + [
            pl.BlockSpec(memory_space=pl.ANY),
            pl.BlockSpec((1, d, ft), lambda e, f: (e, 0, f)),
            pl.BlockSpec((1, d, ft), lambda e, f: (e, 0, f)),
            pl.BlockSpec((1, ft, d), lambda e, f: (e, f, 0)),
        ],
        out_specs=pl.BlockSpec((batch, 1, cap, d), lambda e, f: (0, e, 0, 0)),
        out_shape=jax.ShapeDtypeStruct((batch, ne, cap, d), BF16),
        scratch_shapes=[
            pltpu.VMEM((2, d // LANES, batch * cap, LANES), F32),
            pltpu.VMEM((batch * cap, d), BF16),
            pltpu.VMEM((batch * cap, d), F32),
            pltpu.SemaphoreType.DMA((2,)),
        ],
        compiler_params=_params(("arbitrary", "arbitrary")),
        name="expert_ffn",
    )(*([idx3] * (2 * batch)), h3, w1, w3, w2)


def _combine_body(base_ref, pos_ref, gsel_ref, h_ref, y_ref, g_ref, b_ref, o_ref, win_ref, sem,
                  *, nb, tb, cap, win, wsm):
    bi = pl.program_id(0)
    j = pl.program_id(1)
    step = bi * nb + j
    nsteps = pl.num_programs(0) * nb
    slot = step % 2

    def w0_of(s, e, w):
        base = base_ref[s * N_EXPERTS + e]
        return pl.multiple_of(jnp.minimum((base // 16) * 16, cap - w), 16)

    def fits_small(s):
        ok = None
        for e in range(N_EXPERTS):
            nxt = base_ref[jnp.minimum(s + 1, nsteps - 1) * N_EXPERTS + e]
            end = jnp.where(s % nb == nb - 1, cap, nxt)
            ok_e = end <= w0_of(s, e, wsm) + wsm
            ok = ok_e if ok is None else ok & ok_e
        return ok

    def copies(s, sl, w):
        b_s = s // nb
        return [pltpu.make_async_copy(y_ref.at[b_s, e, pl.ds(w0_of(s, e, w), w), :],
                                      win_ref.at[sl, e, pl.ds(0, w), :], sem.at[sl]) for e in range(N_EXPERTS)]

    def start(s, sl):
        small = fits_small(s)

        @pl.when(small)
        def _():
            for cp in copies(s, sl, wsm):
                cp.start()

        @pl.when(jnp.logical_not(small))
        def _():
            for cp in copies(s, sl, win):
                cp.start()

    @pl.when(step == 0)
    def _():
        start(step, slot)

    @pl.when(step + 1 < nsteps)
    def _():
        start(step + 1, 1 - slot)

    def finish(w):
        for cp in copies(step, slot, w):
            cp.wait()
        p = pos_ref[...]
        gs = gsel_ref[...]
        lane = lax.broadcasted_iota(I32, (tb, w), 1)
        acc = jnp.zeros((tb, D_MODEL), F32)
        for e in range(N_EXPERTS):
            rel = p[:, e:e + 1] - w0_of(step, e, w)
            s_e = jnp.where(rel == lane, gs[:, e:e + 1], 0.0).astype(BF16)
            acc = acc + _dot(s_e, win_ref[slot, e, 0:w, :])
        r = DEEPNORM_ALPHA * h_ref[...] + acc
        o_ref[...] = _layer_norm(r, g_ref[...], b_ref[...])

    small_now = fits_small(step)

    @pl.when(small_now)
    def _():
        finish(wsm)

    @pl.when(jnp.logical_not(small_now))
    def _():
        finish(win)


def _combine(base_flat, pos, gsel, h, y, g, b, batch, seq, cap, tb, wsm=48):
    nb = seq // tb
    win = tb + 16
    return pl.pallas_call(
        functools.partial(_combine_body, nb=nb, tb=tb, cap=cap, win=win, wsm=wsm),
        grid_spec=pltpu.PrefetchScalarGridSpec(
            num_scalar_prefetch=1,
            grid=(batch, nb),
            in_specs=[
                pl.BlockSpec((tb, LANES), lambda bi, j, base: (bi * nb + j, 0)),
                pl.BlockSpec((tb, LANES), lambda bi, j, base: (bi * nb + j, 0)),
                pl.BlockSpec((tb, D_MODEL), lambda bi, j, base: (bi * nb + j, 0)),
                pl.BlockSpec(memory_space=pl.ANY),
                pl.BlockSpec((1, D_MODEL), lambda bi, j, base: (0, 0)),
                pl.BlockSpec((1, D_MODEL), lambda bi, j, base: (0, 0)),
            ],
            out_specs=pl.BlockSpec((tb, D_MODEL), lambda bi, j, base: (bi * nb + j, 0)),
            scratch_shapes=[pltpu.VMEM((2, N_EXPERTS, win, D_MODEL), BF16), pltpu.SemaphoreType.DMA((2,))],
        ),
        out_shape=jax.ShapeDtypeStruct((batch * seq, D_MODEL), F32),
        compiler_params=_params(("arbitrary", "arbitrary")),
        name="combine_ln2",
    )(base_flat, pos, gsel, h, y, g, b)


def _pad_lanes(v, width=LANES):
    return jnp.pad(v, [(0, 0)] * (v.ndim - 1) + [(0, width - v.shape[-1])])


def _hi_lo(w):
    hi = w.astype(BF16)
    lo = (w - hi.astype(F32)).astype(BF16)
    return jnp.concatenate([hi, lo], axis=1)


def _layer(x, w_in, conv_w, conv_b, ssd_a_log_f, ssd_a_log_b, ssd_dt_bias_f, ssd_dt_bias_b, ssd_d, ssd_norm_g,
           w_out_ssd, s5_a_re_f, s5_a_im_f, s5_log_step_f, s5_a_re_b, s5_a_im_b, s5_log_step_b, s5_b_re, s5_b_im,
           s5_c_re, s5_c_im, s5_d, w_glu_a, w_glu_b, w_o, ln1_g, ln1_b, w_router, w_e1, w_e3, w_e2, ln2_g, ln2_b):
    batch, seq, d = x.shape
    n = batch * seq
    x2 = x.reshape(n, d)
    c0, c1, c2, c3 = SSD_INNER, SSD_INNER + XBC_DIM, SSD_INNER + XBC_DIM + 2 * SSD_HEADS, \
        SSD_INNER + XBC_DIM + 2 * SSD_HEADS + S5_WIDTH
    wdt = w_in[:, c1:c2]
    wdt = jnp.concatenate([_pad_lanes(wdt[:, :SSD_HEADS]), _pad_lanes(wdt[:, SSD_HEADS:])], axis=1)
    z, xbc, dt, u, gates = _in_proj(x2, w_in[:, :c0].astype(BF16), w_in[:, c0:c1].astype(BF16), _hi_lo(wdt),
                                    w_in[:, c2:c3].astype(BF16), w_in[:, c3:].astype(BF16), tb=512)

    t = S5_CHUNK
    nc = seq // t
    nr = nc // S5_SEGS
    u2 = u.reshape(batch, S5_SEGS, nr, t * S5_WIDTH).transpose(0, 2, 1, 3).reshape(batch * nc, t * S5_WIDTH)
    mf = _s5_mats(s5_a_re_f, s5_a_im_f, s5_log_step_f, s5_b_re, s5_b_im, s5_c_re, s5_c_im, False)
    mb = _s5_mats(s5_a_re_b, s5_a_im_b, s5_log_step_b, s5_b_re, s5_b_im, s5_c_re, s5_c_im, True)
    toep = _tile_toep(mf[0] + mb[0]).astype(BF16)
    win = jnp.stack([_tile_in(m) for m in (mf[1], mf[2], mb[1], mb[2])], axis=1).astype(BF16)
    wout = jnp.stack([_tile_out(m) for m in (mf[3], mf[4], mb[3], mb[4])], axis=1).astype(BF16)
    sw = S5_GPT * S5_STATE

    def powc(lr, li, k):
        mag = jnp.exp(lr * k)
        return (mag * jnp.cos(li * k)).reshape(S5_TILES, sw), (mag * jnp.sin(li * k)).reshape(S5_TILES, sw)

    coef = jnp.stack([*powc(*mf[5], float(t)), *powc(*mf[5], float(t * nr)),
                      *powc(*mb[5], float(t)), *powc(*mb[5], float(t * nr))], axis=1)
    ys = _s5(u2, toep, win, wout, coef, batch, nc)
    s5y = jnp.stack(ys, axis=1).reshape(batch, nr, S5_SEGS, t * S5_WIDTH).transpose(0, 2, 1, 3).reshape(n, S5_WIDTH)

    xs, bm, cm = _conv(xbc, conv_w, conv_b.reshape(1, XBC_DIM), seq, tb=512)

    def ssd_par(a_log, bias):
        p = jnp.zeros((8, LANES), F32)
        return p.at[0, :SSD_HEADS].set(-jnp.exp(a_log)).at[1, :SSD_HEADS].set(bias)

    yb = _ssd(xs, bm, cm, dt, ssd_par(ssd_a_log_b, ssd_dt_bias_b), batch, seq, True)
    yf = _ssd(xs, bm, cm, dt, ssd_par(ssd_a_log_f, ssd_dt_bias_f), batch, seq, False)

    wr = _hi_lo(_pad_lanes(w_router))
    h, lg = _mid(yf, yb, xs, z, s5y, u, gates, x2, jnp.repeat(ssd_d, SSD_HEADDIM).reshape(1, SSD_INNER),
                 ssd_norm_g.reshape(1, SSD_INNER), w_out_ssd.astype(BF16), s5_d.reshape(1, S5_WIDTH),
                 w_glu_a.astype(BF16), w_glu_b.astype(BF16), w_o.astype(BF16), ln1_g.reshape(1, d),
                 ln1_b.reshape(1, d), wr, tb=256)

    cap = CAPACITY_FACTOR * seq // N_EXPERTS
    tbk = 128
    gsel, pos, base = _topk(lg, batch, seq, cap, tbk)
    base_flat = base[:, :, :N_EXPERTS].reshape(-1)
    idx = _compact(base_flat, pos, batch, seq, cap, tbk)
    idx3 = idx[:, :, :cap // LANES, :].reshape(batch * N_EXPERTS, 1, cap)
    y = _ffn(idx3, h.reshape(n, d // LANES, LANES), w_e1, w_e3, w_e2, batch, seq, cap, ft=256)
    out = _combine(base_flat, pos, gsel, h, y, ln2_g.reshape(1, d), ln2_b.reshape(1, d), batch, seq, cap, tbk)
    return out.reshape(batch, seq, d)


def kernel(x, w_in, conv_w, conv_b, ssd_a_log_f, ssd_a_log_b, ssd_dt_bias_f, ssd_dt_bias_b, ssd_d, ssd_norm_g, w_out_ssd, s5_a_re_f, s5_a_im_f, s5_log_step_f, s5_a_re_b, s5_a_im_b, s5_log_step_b, s5_b_re, s5_b_im, s5_c_re, s5_c_im, s5_d, w_glu_a, w_glu_b, w_o, ln1_g, ln1_b, w_router, w_e1, w_e3, w_e2, ln2_g, ln2_b):
    h = x
    for i in range(DEPTH):
        h = _layer(h, w_in[i], conv_w[i], conv_b[i], ssd_a_log_f[i], ssd_a_log_b[i], ssd_dt_bias_f[i],
                   ssd_dt_bias_b[i], ssd_d[i], ssd_norm_g[i], w_out_ssd[i], s5_a_re_f[i], s5_a_im_f[i],
                   s5_log_step_f[i], s5_a_re_b[i], s5_a_im_b[i], s5_log_step_b[i], s5_b_re[i], s5_b_im[i],
                   s5_c_re[i], s5_c_im[i], s5_d[i], w_glu_a[i], w_glu_b[i], w_o[i], ln1_g[i], ln1_b[i],
                   w_router[i], w_e1[i], w_e3[i], w_e2[i], ln2_g[i], ln2_b[i])
    return h
```

```python
import functools
import math

import jax
import jax.numpy as jnp
import numpy as np
from jax import lax
from jax.experimental import pallas as pl
from jax.experimental.pallas import tpu as pltpu

F32 = jnp.float32
BF16 = jnp.bfloat16
I32 = jnp.int32

D_MODEL = 1024
SSD_HEADDIM = 64
SSD_HEADS = 24
SSD_INNER = SSD_HEADS * SSD_HEADDIM
SSD_GROUPS = 4
SSD_STATE = 128
SSD_CONV = 5
SSD_BC = 2 * SSD_GROUPS * SSD_STATE
XBC_DIM = SSD_INNER + SSD_BC
S5_GROUP = 16
S5_WIDTH = 768
S5_GROUPS = S5_WIDTH // S5_GROUP
S5_STATE = 64
N_EXPERTS = 16
EXPERT_FF = 2816
CAPACITY_FACTOR = 2
DEPTH = 1
DEEPNORM_ALPHA = (2.0 * DEPTH) ** 0.25
LN_EPS = 1e-5
RMS_EPS = 1e-5
LOG2E = 1.4426950408889634

LANES = 128
S5_CHUNK = 8
S5_SEGS = 8
SSD_Q = 128
CONV_HALO = 16
VMEM_LIMIT = 56 * 1024 * 1024


def _dot(a, b):
    return jnp.dot(a, b, preferred_element_type=F32)


def _split3(v):
    v1 = v.astype(BF16)
    r1 = v - v1.astype(F32)
    v2 = r1.astype(BF16)
    v3 = (r1 - v2.astype(F32)).astype(BF16)
    return v1, v2, v3


def _dot_exact_lhs(m_bf16, v):
    v1, v2, v3 = _split3(v)
    return _dot(m_bf16, v1) + _dot(m_bf16, v2) + _dot(m_bf16, v3)


def _sigmoid(x):
    return 1.0 / (1.0 + jnp.exp(-x))


def _params(sem, **kw):
    return pltpu.CompilerParams(dimension_semantics=sem, vmem_limit_bytes=VMEM_LIMIT, **kw)


def _in_proj_body(x_ref, wz_ref, wxbc_ref, wdt_ref, wu_ref, wg_ref, z_ref, xbc_ref, dt_ref, u_ref, g_ref):
    x = x_ref[...]
    xh = x.astype(BF16)
    xl = (x - xh.astype(F32)).astype(BF16)
    z_ref[...] = _dot(xh, wz_ref[...]).astype(z_ref.dtype)
    xbc_ref[...] = _dot(xh, wxbc_ref[...]).astype(xbc_ref.dtype)
    u_ref[...] = _dot(xh, wu_ref[...]).astype(u_ref.dtype)
    g_ref[...] = _dot(xh, wg_ref[...]).astype(g_ref.dtype)
    d = _dot(xh, wdt_ref[...])
    nd = dt_ref.shape[-1]
    dt_ref[...] = d[:, :nd] + d[:, nd:] + _dot(xl, wdt_ref[:, :nd])


def _in_proj(x2, wz, wxbc, wdt, wu, wg, tb):
    n = x2.shape[0]
    full = lambda w: pl.BlockSpec(w.shape, lambda i: (0, 0))
    row = lambda c: pl.BlockSpec((tb, c), lambda i: (i, 0))
    nd = wdt.shape[1] // 2
    return pl.pallas_call(
        _in_proj_body,
        grid=(n // tb,),
        in_specs=[row(D_MODEL), full(wz), full(wxbc), full(wdt), full(wu), full(wg)],
        out_specs=[row(SSD_INNER), row(XBC_DIM), row(nd), row(S5_WIDTH), row(2 * D_MODEL)],
        out_shape=[
            jax.ShapeDtypeStruct((n, SSD_INNER), BF16),
            jax.ShapeDtypeStruct((n, XBC_DIM), BF16),
            jax.ShapeDtypeStruct((n, nd), F32),
            jax.ShapeDtypeStruct((n, S5_WIDTH), BF16),
            jax.ShapeDtypeStruct((n, 2 * D_MODEL), BF16),
        ],
        compiler_params=_params(("arbitrary",)),
        name="in_proj",
    )(x2, wz, wxbc, wdt, wu, wg)


def _conv_body(cur_ref, prev_ref, next_ref, w_ref, b_ref, xs_ref, bm_ref, cm_ref, ext_ref, *, tb, blocks_per_seq):
    i = pl.program_id(0)
    first = (i % blocks_per_seq) == 0
    last = (i % blocks_per_seq) == blocks_per_seq - 1
    hr = CONV_HALO
    ext_ref[0:hr, :] = jnp.where(first, 0.0, prev_ref[0].astype(F32))
    ext_ref[hr:hr + tb, :] = cur_ref[...].astype(F32)
    ext_ref[hr + tb:2 * hr + tb, :] = jnp.where(last, 0.0, next_ref[0].astype(F32))
    half = SSD_CONV // 2
    cw = 512
    for c0 in range(0, XBC_DIM, cw):
        acc = jnp.broadcast_to(b_ref[:, c0:c0 + cw], (tb, cw))
        for j in range(SSD_CONV):
            acc = acc + ext_ref[hr - half + j:hr - half + j + tb, c0:c0 + cw] * w_ref[j:j + 1, c0:c0 + cw]
        y = (acc * _sigmoid(acc)).astype(BF16)
        if c0 + cw <= SSD_INNER:
            xs_ref[:, c0:c0 + cw] = y
        elif c0 < SSD_INNER + SSD_BC // 2:
            bm_ref[:, c0 - SSD_INNER:c0 - SSD_INNER + cw] = y
        else:
            o = c0 - SSD_INNER - SSD_BC // 2
            cm_ref[:, o:o + cw] = y


def _conv(xbc, conv_w, conv_b, seq, tb):
    n = xbc.shape[0]
    nb = n // tb
    hr = CONV_HALO
    xbc3 = xbc.reshape(n // hr, hr, XBC_DIM)
    r = tb // hr
    return pl.pallas_call(
        functools.partial(_conv_body, tb=tb, blocks_per_seq=seq // tb),
        grid=(nb,),
        in_specs=[
            pl.BlockSpec((tb, XBC_DIM), lambda i: (i, 0)),
            pl.BlockSpec((1, hr, XBC_DIM), lambda i: (jnp.maximum(i * r - 1, 0), 0, 0)),
            pl.BlockSpec((1, hr, XBC_DIM), lambda i: (jnp.minimum((i + 1) * r, n // hr - 1), 0, 0)),
            pl.BlockSpec((SSD_CONV, XBC_DIM), lambda i: (0, 0)),
            pl.BlockSpec((1, XBC_DIM), lambda i: (0, 0)),
        ],
        out_specs=[
            pl.BlockSpec((tb, SSD_INNER), lambda i: (i, 0)),
            pl.BlockSpec((tb, SSD_BC // 2), lambda i: (i, 0)),
            pl.BlockSpec((tb, SSD_BC // 2), lambda i: (i, 0)),
        ],
        out_shape=[
            jax.ShapeDtypeStruct((n, SSD_INNER), BF16),
            jax.ShapeDtypeStruct((n, SSD_BC // 2), BF16),
            jax.ShapeDtypeStruct((n, SSD_BC // 2), BF16),
        ],
        scratch_shapes=[pltpu.VMEM((tb + 2 * hr, XBC_DIM), F32)],
        compiler_params=_params(("arbitrary",)),
        name="conv_silu",
    )(xbc, xbc3, xbc3, conv_w, conv_b)


def _ssd_body(xs_ref, bm_ref, cm_ref, dt_ref, par_ref, y_ref, st_ref, *, reverse, q):
    c = pl.program_id(1)

    @pl.when(c == 0)
    def _():
        st_ref[...] = jnp.zeros_like(st_ref)

    a = par_ref[0:1, :]
    bias = par_ref[1:2, :]
    dt = jax.nn.softplus(dt_ref[...] + bias)
    dta = dt * a
    ri = lax.broadcasted_iota(I32, (q, q), 0)
    ci = lax.broadcasted_iota(I32, (q, q), 1)
    mask = (ci >= ri) if reverse else (ci <= ri)
    cum = _dot_exact_lhs(mask.astype(BF16), dta)
    total = cum[0:1, :] if reverse else cum[q - 1:q, :]
    cum2 = cum * LOG2E
    tot2 = total * LOG2E
    rowp = cum2.T - jnp.log2(dt.T)
    ecum = jnp.exp2(cum2)
    wst = dt * jnp.exp2(tot2 - cum2)
    etot = jnp.exp2(tot2)
    lane = lax.broadcasted_iota(I32, (q, LANES), 1)
    lo = lane < SSD_HEADDIM
    hpg = SSD_HEADS // SSD_GROUPS
    gw = hpg * SSD_HEADDIM
    lane_g = lax.broadcasted_iota(I32, (1, gw), 1) // SSD_HEADDIM
    for g in range(SSD_GROUPS):
        bg = bm_ref[:, SSD_STATE * g:SSD_STATE * (g + 1)]
        cg = cm_ref[:, SSD_STATE * g:SSD_STATE * (g + 1)]
        cb = lax.dot_general(cg, bg, (((1,), (1,)), ((), ())), preferred_element_type=F32)
        st = st_ref[g]
        yoff = _dot(cg, st.astype(BF16))
        bgt = bg.astype(F32).T.astype(BF16)
        xw_parts = []
        for jp in range(hpg // 2):
            h0 = hpg * g + 2 * jp
            xt = xs_ref[:, SSD_HEADDIM * h0:SSD_HEADDIM * h0 + LANES]
            ws = []
            for hh in (h0, h0 + 1):
                seg = cum2[:, hh:hh + 1] - rowp[hh:hh + 1, :]
                e = jnp.exp2(jnp.where(mask, seg, -jnp.inf))
                ws.append((cb * e).astype(BF16))
            wp = jnp.concatenate(ws, axis=1)
            zero = jnp.zeros_like(xt)
            rhs = jnp.concatenate([jnp.where(lo, xt, zero), jnp.where(lo, zero, xt)], axis=0)
            yd = _dot(wp, rhs)
            ec = jnp.where(lo, ecum[:, h0:h0 + 1], ecum[:, h0 + 1:h0 + 2])
            y_ref[:, SSD_HEADDIM * h0:SSD_HEADDIM * h0 + LANES] = (
                yd + yoff[:, LANES * jp:LANES * (jp + 1)] * ec).astype(y_ref.dtype)
            wc = jnp.where(lo, wst[:, h0:h0 + 1], wst[:, h0 + 1:h0 + 2])
            xw_parts.append((xt.astype(F32) * wc).astype(BF16))
        xw = jnp.concatenate(xw_parts, axis=1)
        snew = _dot(bgt, xw)
        eg = jnp.zeros((1, gw), F32)
        for j in range(hpg):
            eg = jnp.where(lane_g == j, etot[:, hpg * g + j:hpg * g + j + 1], eg)
        st_ref[g] = st * eg + snew


def _ssd(xs, bm, cm, dt, par, batch, seq, reverse):
    n = xs.shape[0]
    q = SSD_Q
    nc = seq // q
    d = 1 if reverse else 0

    def blk(b, c):
        return b * nc + (nc - 1 - c if reverse else c)

    gw = (SSD_HEADS // SSD_GROUPS) * SSD_HEADDIM
    return pl.pallas_call(
        functools.partial(_ssd_body, reverse=reverse, q=q),
        grid=(batch, nc),
        in_specs=[
            pl.BlockSpec((q, SSD_INNER), lambda b, c: (blk(b, c), 0)),
            pl.BlockSpec((q, SSD_BC // 2), lambda b, c: (blk(b, c), 0)),
            pl.BlockSpec((q, SSD_BC // 2), lambda b, c: (blk(b, c), 0)),
            pl.BlockSpec((q, LANES), lambda b, c: (blk(b, c), d)),
            pl.BlockSpec((8, LANES), lambda b, c: (0, 0)),
        ],
        out_specs=pl.BlockSpec((q, SSD_INNER), lambda b, c: (blk(b, c), 0)),
        out_shape=jax.ShapeDtypeStruct((n, SSD_INNER), BF16),
        scratch_shapes=[pltpu.VMEM((SSD_GROUPS, SSD_STATE, gw), F32)],
        compiler_params=_params(("arbitrary", "arbitrary")),
        name="ssd_bwd" if reverse else "ssd_fwd",
    )(xs, bm, cm, dt, par)


def _s5_mats(a_re, a_im, log_step, b_re, b_im, c_re, c_im, reverse):
    t = S5_CHUNK
    hp = lax.Precision.HIGHEST
    step = jnp.exp(log_step.astype(F32))[:, None]
    lr = a_re.astype(F32) * step
    li = a_im.astype(F32) * step
    pw = jnp.arange(t + 1, dtype=F32)[None, :, None]
    mag = jnp.exp(lr[:, None, :] * pw)
    pr = mag * jnp.cos(li[:, None, :] * pw)
    pi = mag * jnp.sin(li[:, None, :] * pw)
    ar, ai = pr[:, 1], pi[:, 1]
    den = a_re.astype(F32) ** 2 + a_im.astype(F32) ** 2
    qr = ((ar - 1.0) * a_re + ai * a_im) / den
    qi = (ai * a_re - (ar - 1.0) * a_im) / den
    br = qr[..., None] * b_re - qi[..., None] * b_im
    bi = qr[..., None] * b_im + qi[..., None] * b_re
    cpr = c_re[:, None] * pr[:, :t, None, :] - c_im[:, None] * pi[:, :t, None, :]
    cpi = c_re[:, None] * pi[:, :t, None, :] + c_im[:, None] * pr[:, :t, None, :]
    m = jnp.einsum("gtkp,gpj->gtkj", cpr, br, precision=hp) - jnp.einsum("gtkp,gpj->gtkj", cpi, bi, precision=hp)
    s_i = np.arange(t)[:, None, None]
    t_i = np.arange(t)[None, :, None]
    lag = np.arange(t)[None, None, :]
    pick = ((s_i - t_i) if reverse else (t_i - s_i)) == lag
    mt = jnp.einsum("stu,gukj->gsjtk", jnp.asarray(pick, F32), m, precision=hp)
    toep = mt.reshape(S5_GROUPS, t * S5_GROUP, t * S5_GROUP)
    e_in = (jnp.arange(t) if reverse else (t - 1 - jnp.arange(t)))
    wr = pr[:, e_in][..., None] * br[:, None] - pi[:, e_in][..., None] * bi[:, None]
    wi = pr[:, e_in][..., None] * bi[:, None] + pi[:, e_in][..., None] * br[:, None]
    win_re = wr.transpose(0, 1, 3, 2).reshape(S5_GROUPS, t * S5_GROUP, S5_STATE)
    win_im = wi.transpose(0, 1, 3, 2).reshape(S5_GROUPS, t * S5_GROUP, S5_STATE)
    e_out = (t - jnp.arange(t)) if reverse else (jnp.arange(t) + 1)
    gr = c_re[:, None] * pr[:, e_out, None, :] - c_im[:, None] * pi[:, e_out, None, :]
    gi = c_re[:, None] * pi[:, e_out, None, :] + c_im[:, None] * pr[:, e_out, None, :]
    wout_re = gr.transpose(0, 3, 1, 2).reshape(S5_GROUPS, S5_STATE, t * S5_GROUP)
    wout_im = (-gi).transpose(0, 3, 1, 2).reshape(S5_GROUPS, S5_STATE, t * S5_GROUP)
    return toep, win_re, win_im, wout_re, wout_im, (lr, li)


S5_GPT = LANES // S5_GROUP
S5_TILES = S5_WIDTH // LANES


def _tile_toep(w):
    t = w.shape[1] // S5_GROUP
    w6 = w.reshape(S5_TILES, S5_GPT, t, S5_GROUP, t, S5_GROUP).transpose(0, 2, 1, 3, 4, 5)
    eye = jnp.eye(S5_GPT, dtype=w.dtype)
    out = w6[:, :, :, :, :, None, :] * eye[None, None, :, None, None, :, None]
    return out.reshape(S5_TILES, t * LANES, t * LANES)


def _tile_in(w):
    t = w.shape[1] // S5_GROUP
    w5 = w.reshape(S5_TILES, S5_GPT, t, S5_GROUP, w.shape[2]).transpose(0, 2, 1, 3, 4)
    eye = jnp.eye(S5_GPT, dtype=w.dtype)
    out = w5[:, :, :, :, None, :] * eye[None, None, :, None, :, None]
    return out.reshape(S5_TILES, t * LANES, S5_GPT * w.shape[2])


def _tile_out(w):
    t = w.shape[2] // S5_GROUP
    w5 = w.reshape(S5_TILES, S5_GPT, w.shape[1], t, S5_GROUP)
    eye = jnp.eye(S5_GPT, dtype=w.dtype)
    out = w5[:, :, :, :, None, :] * eye[None, :, None, None, :, None]
    return out.reshape(S5_TILES, S5_GPT * w.shape[1], t * LANES)


def _cmul(xr, xi, ar, ai):
    return xr * ar - xi * ai, xr * ai + xi * ar


def _s5_body(*refs, nt, nseg_rows):
    u_refs = refs[:nt]
    toep_ref, win_ref, wout_ref, co_ref = refs[nt:nt + 4]
    y_ref = refs[nt + 4]
    s_ref = refs[nt + 5]
    x_ref = s_ref
    up = jnp.concatenate([r[...] for r in u_refs], axis=1)
    co = co_ref[0]
    nr = nseg_rows
    sw = co.shape[-1]
    nlt = sw // LANES
    for k in range(4):
        sk = _dot(up, win_ref[0, k])
        for lt in range(nlt):
            s_ref[k, lt] = sk[:, LANES * lt:LANES * (lt + 1)]
    zero = jnp.zeros((S5_SEGS, sw), F32)

    def scan(kr, ki, ar, ai, asr, asi, reverse):
        def rows(i):
            r = (nr - 1 - i) if reverse else i
            return pl.ds(r, S5_SEGS, stride=nr)

        def ld(k, i):
            return jnp.concatenate([s_ref[k, lt, rows(i), :] for lt in range(nlt)], axis=1)

        def st(k, i, v):
            for lt in range(nlt):
                x_ref[k, lt, rows(i), :] = v[:, LANES * lt:LANES * (lt + 1)]

        def p1(i, c):
            xr, xi = _cmul(c[0], c[1], ar, ai)
            return xr + ld(kr, i), xi + ld(ki, i)

        er, ei = lax.fori_loop(0, nr, p1, (zero, zero))
        order = range(S5_SEGS - 1, -1, -1) if reverse else range(S5_SEGS)
        cr = jnp.zeros((1, sw), F32)
        ci = jnp.zeros((1, sw), F32)
        crs, cis = [None] * S5_SEGS, [None] * S5_SEGS
        for j in order:
            crs[j], cis[j] = cr, ci
            nr_, ni_ = _cmul(cr, ci, asr, asi)
            cr, ci = nr_ + er[j:j + 1], ni_ + ei[j:j + 1]
        c0 = (jnp.concatenate(crs, axis=0), jnp.concatenate(cis, axis=0))

        def p2(i, c):
            sr = ld(kr, i)
            si = ld(ki, i)
            st(kr, i, c[0])
            st(ki, i, c[1])
            xr, xi = _cmul(c[0], c[1], ar, ai)
            return xr + sr, xi + si

        lax.fori_loop(0, nr, p2, c0)

    scan(0, 1, co[0:1], co[1:2], co[2:3], co[3:4], False)
    scan(2, 3, co[4:5], co[5:6], co[6:7], co[7:8], True)
    y = _dot(up, toep_ref[0])
    for k in range(4):
        xk = jnp.concatenate([x_ref[k, lt] for lt in range(nlt)], axis=1)
        y = y + _dot(xk.astype(BF16), wout_ref[0, k])
    for t in range(nt):
        y_ref[t] = y[:, LANES * t:LANES * (t + 1)].astype(y_ref.dtype)


def _s5(u2, toep, win, wout, coef, batch, nc):
    nt = u2.shape[1] // S5_WIDTH
    sw = win.shape[-1]
    col = lambda t: pl.BlockSpec((nc, LANES), lambda j, b, t=t: (b, S5_TILES * t + j))
    return pl.pallas_call(
        functools.partial(_s5_body, nt=nt, nseg_rows=nc // S5_SEGS),
        grid=(S5_TILES, batch),
        in_specs=[col(t) for t in range(nt)] + [
            pl.BlockSpec((1, nt * LANES, nt * LANES), lambda j, b: (j, 0, 0)),
            pl.BlockSpec((1, 4, nt * LANES, sw), lambda j, b: (j, 0, 0, 0)),
            pl.BlockSpec((1, 4, sw, nt * LANES), lambda j, b: (j, 0, 0, 0)),
            pl.BlockSpec((1, 8, sw), lambda j, b: (j, 0, 0)),
        ],
        out_specs=pl.BlockSpec((nt, nc, LANES), lambda j, b: (0, b, j)),
        out_shape=jax.ShapeDtypeStruct((nt, batch * nc, S5_WIDTH), BF16),
        scratch_shapes=[pltpu.VMEM((4, sw // LANES, nc, LANES), F32)],
        compiler_params=_params(("arbitrary", "arbitrary")),
        name="s5_scan",
    )(*([u2] * nt), toep, win, wout, coef)


def _layer_norm(r, g, b):
    mu = jnp.mean(r, axis=-1, keepdims=True)
    d = r - mu
    var = jnp.mean(d * d, axis=-1, keepdims=True)
    return d * lax.rsqrt(var + LN_EPS) * g + b


def _mid_body(yf_ref, yb_ref, xs_ref, z_ref, s5_ref, u_ref, g_ref, x_ref, dexp_ref, ng_ref, wout_ref, s5d_ref,
              wga_ref, wgb_ref, wo_ref, l1g_ref, l1b_ref, wr_ref, h_ref, lg_ref):
    y = yf_ref[...].astype(F32) + yb_ref[...].astype(F32) + xs_ref[...].astype(F32) * dexp_ref[...]
    z = z_ref[...].astype(F32)
    y = y * (z * _sigmoid(z))
    gw = SSD_INNER // SSD_GROUPS
    parts = []
    for g in range(SSD_GROUPS):
        yg = y[:, gw * g:gw * (g + 1)]
        ms = jnp.mean(yg * yg, axis=-1, keepdims=True)
        parts.append(yg * lax.rsqrt(ms + RMS_EPS))
    yn = (jnp.concatenate(parts, axis=1) * ng_ref[...]).astype(BF16)
    ya = _dot(yn, wout_ref[...])
    v = s5_ref[...].astype(F32) + u_ref[...].astype(F32) * s5d_ref[...]
    v = jax.nn.gelu(v).astype(BF16)
    yb = _dot(v, wga_ref[...]) * _sigmoid(_dot(v, wgb_ref[...]))
    gt = g_ref[...].astype(F32)
    mix = _sigmoid(gt[:, :D_MODEL]) * ya + _sigmoid(gt[:, D_MODEL:]) * yb
    r = DEEPNORM_ALPHA * x_ref[...] + _dot(mix.astype(BF16), wo_ref[...])
    h = _layer_norm(r, l1g_ref[...], l1b_ref[...])
    h_ref[...] = h
    hh = h.astype(BF16)
    hl = (h - hh.astype(F32)).astype(BF16)
    d = _dot(hh, wr_ref[...])
    lg_ref[...] = d[:, :LANES] + d[:, LANES:] + _dot(hl, wr_ref[:, :LANES])


def _mid(yf, yb, xs, z, s5y, u, gates, x2, dexp, ng, wout, s5d, wga, wgb, wo, l1g, l1b, wr, tb):
    n = x2.shape[0]
    row = lambda a: pl.BlockSpec((tb, a.shape[1]), lambda i: (i, 0))
    full = lambda a: pl.BlockSpec(a.shape, lambda i: (0, 0))
    acts = [yf, yb, xs, z, s5y, u, gates, x2]
    wts = [dexp, ng, wout, s5d, wga, wgb, wo, l1g, l1b, wr]
    return pl.pallas_call(
        _mid_body,
        grid=(n // tb,),
        in_specs=[row(a) for a in acts] + [full(w) for w in wts],
        out_specs=[pl.BlockSpec((tb, D_MODEL), lambda i: (i, 0)), pl.BlockSpec((tb, LANES), lambda i: (i, 0))],
        out_shape=[jax.ShapeDtypeStruct((n, D_MODEL), F32), jax.ShapeDtypeStruct((n, LANES), F32)],
        compiler_params=_params(("arbitrary",)),
        name="merge_ln1",
    )(*acts, *wts)


def _topk_body(lg_ref, gsel_ref, pos_ref, base_ref, aff_ref, *, seq, cap, tb):
    lane = lax.broadcasted_iota(I32, (seq, LANES), 1)
    valid = lane < N_EXPERTS
    lg = jnp.where(valid, lg_ref[...], -jnp.inf)
    m = jnp.max(lg, axis=1, keepdims=True)
    e = jnp.exp(lg - m)
    aff_ref[...] = e / jnp.sum(e, axis=1, keepdims=True)

    def count_ge(t_bits):
        t = pltpu.bitcast(t_bits, F32)
        return jnp.sum((aff_ref[...] >= t).astype(I32), axis=0, keepdims=True)

    def bs(_, c):
        lo, hi = c
        mid = lo + ((hi - lo) >> 1)
        ge = count_ge(mid) >= cap
        return jnp.where(ge, mid, lo), jnp.where(ge, hi, mid)

    lo0 = jnp.zeros((1, LANES), I32)
    hi0 = jnp.full((1, LANES), 0x3F800001, I32)
    thr_bits, _ = lax.fori_loop(0, 31, bs, (lo0, hi0))
    thr = pltpu.bitcast(thr_bits, F32)
    nxt = pltpu.bitcast(thr_bits + 1, F32)
    n_gt = count_ge(thr_bits + 1)
    need = (cap - n_gt).astype(F32)
    ri = lax.broadcasted_iota(I32, (tb, tb), 0)
    ci = lax.broadcasted_iota(I32, (tb, tb), 1)
    tri = (ci < ri).astype(BF16)
    vrow = lax.broadcasted_iota(I32, (1, LANES), 1) < N_EXPERTS

    def blk(j, c):
        ceq, csel = c
        rows = pl.ds(pl.multiple_of(j * tb, tb), tb)
        a = aff_ref[rows, :]
        gt = a >= nxt
        eq = (a >= thr) & jnp.logical_not(gt)
        eqc = ceq + _dot(tri, eq.astype(BF16))
        sel = (gt | (eq & (eqc < need))) & vrow
        self = sel.astype(F32)
        pos = csel + _dot(tri, self.astype(BF16))
        pos_ref[rows, :] = jnp.where(sel, pos, -1.0).astype(I32)
        gsel_ref[rows, :] = jnp.where(sel, a, 0.0)
        base_ref[0, pl.ds(j, 1), :] = csel.astype(I32)
        return (ceq + jnp.sum(eq.astype(F32), axis=0, keepdims=True),
                csel + jnp.sum(self, axis=0, keepdims=True))

    z = jnp.zeros((1, LANES), F32)
    lax.fori_loop(0, seq // tb, blk, (z, z))


def _topk(lg, batch, seq, cap, tb):
    nb = seq // tb
    return pl.pallas_call(
        functools.partial(_topk_body, seq=seq, cap=cap, tb=tb),
        grid=(batch,),
        in_specs=[pl.BlockSpec((seq, LANES), lambda b: (b, 0))],
        out_specs=[
            pl.BlockSpec((seq, LANES), lambda b: (b, 0)),
            pl.BlockSpec((seq, LANES), lambda b: (b, 0)),
            pl.BlockSpec((1, nb, LANES), lambda b: (b, 0, 0)),
        ],
        out_shape=[
            jax.ShapeDtypeStruct((batch * seq, LANES), F32),
            jax.ShapeDtypeStruct((batch * seq, LANES), I32),
            jax.ShapeDtypeStruct((batch, nb, LANES), I32),
        ],
        scratch_shapes=[pltpu.VMEM((seq, LANES), F32)],
        compiler_params=_params(("arbitrary",)),
        name="topk_select",
    )(lg)


def _compact_body(base_ref, pos_ref, idx_ref, *, nb, tb):
    b = pl.program_id(0)
    idx_ref[...] = jnp.zeros_like(idx_ref)
    lane = lax.broadcasted_iota(I32, (tb, LANES), 1)
    trow = lax.broadcasted_iota(I32, (tb, LANES), 0)

    def blk(j, carry):
        rows = pl.ds(pl.multiple_of(j * tb, tb), tb)
        p = pos_ref[rows, :]
        tok = trow + j * tb
        for e in range(N_EXPERTS):
            wb = base_ref[(b * nb + j) * N_EXPERTS + e] // LANES
            rel = p[:, e:e + 1] - wb * LANES
            lo = jnp.sum(jnp.where(rel == lane, tok, 0), axis=0, keepdims=True)
            hi = jnp.sum(jnp.where(rel - LANES == lane, tok, 0), axis=0, keepdims=True)
            idx_ref[0, e, pl.ds(wb, 1), :] += lo
            idx_ref[0, e, pl.ds(wb + 1, 1), :] += hi
        return carry

    lax.fori_loop(0, nb, blk, 0)


def _compact(base_flat, pos, batch, seq, cap, tb):
    nb = seq // tb
    rows = cap // LANES + 8
    return pl.pallas_call(
        functools.partial(_compact_body, nb=nb, tb=tb),
        grid_spec=pltpu.PrefetchScalarGridSpec(
            num_scalar_prefetch=1,
            grid=(batch,),
            in_specs=[pl.BlockSpec((seq, LANES), lambda b, base: (b, 0))],
            out_specs=pl.BlockSpec((1, N_EXPERTS, rows, LANES), lambda b, base: (b, 0, 0, 0)),
        ),
        out_shape=jax.ShapeDtypeStruct((batch, N_EXPERTS, rows, LANES), I32),
        compiler_params=_params(("arbitrary",)),
        name="slot_compact",
    )(base_flat, pos)


def _ffn_body(*refs, nf, nbatch, seq, cap):
    idx_refs = refs[:nbatch]
    idxn_refs = refs[nbatch:2 * nbatch]
    h3_ref, w1_ref, w3_ref, w2_ref, y_ref, xbuf_ref, xb_ref, acc_ref, sem = refs[2 * nbatch:]
    e = pl.program_id(0)
    f = pl.program_id(1)
    ne = pl.num_programs(0)
    slot = e % 2
    nk = D_MODEL // LANES

    def issue(irefs, sl):
        for bi in range(nbatch):
            def body(s, c, bi=bi):
                t = irefs[bi][0, 0, s] + bi * seq
                pltpu.make_async_copy(h3_ref.at[t], xbuf_ref.at[sl, :, bi * cap + s, :], sem.at[sl]).start()
                return c

            lax.fori_loop(0, cap, body, 0, unroll=8)

    @pl.when((f == 0) & (e == 0))
    def _():
        issue(idx_refs, slot)

    @pl.when(f == 0)
    def _():
        pltpu.make_async_copy(xbuf_ref.at[slot], xbuf_ref.at[slot], sem.at[slot]).wait()
        for k in range(nk):
            xb_ref[:, LANES * k:LANES * (k + 1)] = xbuf_ref[slot, k].astype(BF16)
        acc_ref[...] = jnp.zeros_like(acc_ref)

    @pl.when((f == 1) & (e + 1 < ne))
    def _():
        issue(idxn_refs, 1 - slot)

    w1 = w1_ref[0].astype(BF16)
    w3 = w3_ref[0].astype(BF16)
    w2 = w2_ref[0].astype(BF16)
    for bi in range(nbatch):
        rows = slice(bi * cap, (bi + 1) * cap)
        x = xb_ref[rows, :]
        a = _dot(x, w1)
        g = _dot(x, w3)
        hid = (a * _sigmoid(a) * g).astype(BF16)
        acc_ref[rows, :] += _dot(hid, w2)

    @pl.when(f == nf - 1)
    def _():
        for bi in range(nbatch):
            y_ref[bi, 0] = acc_ref[bi * cap:(bi + 1) * cap, :].astype(y_ref.dtype)


def _ffn(idx3, h3, w1, w3, w2, batch, seq, cap, ft):
    ne, d, ff = w1.shape
    nf = ff // ft
    assert nf >= 2
    smem = lambda imap: pl.BlockSpec((1, 1, cap), imap, memory_space=pltpu.SMEM)
    cur = [smem(lambda e, f, bi=bi: (bi * ne + e, 0, 0)) for bi in range(batch)]
    nxt = [smem(lambda e, f, bi=bi: (bi * ne + jnp.minimum(e + 1, ne - 1), 0, 0)) for bi in range(batch)]
    return pl.pallas_call(
        functools.partial(_ffn_body, nf=nf, nbatch=batch, seq=seq, cap=cap),
        grid=(ne, nf),
        in_specs=cur + nxt + [
            pl.BlockSpec(memory_space=pl.ANY),
            pl.BlockSpec((1, d, ft), lambda e, f: (e, 0, f)),
            pl.BlockSpec((1, d, ft), lambda e, f: (e, 0, f)),
            pl.BlockSpec((1, ft, d), lambda e, f: (e, f, 0)),
        ],
        out_specs=pl.BlockSpec((batch, 1, cap, d), lambda e, f: (0, e, 0, 0)),
        out_shape=jax.ShapeDtypeStruct((batch, ne, cap, d), BF16),
        scratch_shapes=[
            pltpu.VMEM((2, d // LANES, batch * cap, LANES), F32),
            pltpu.VMEM((batch * cap, d), BF16),
            pltpu.VMEM((batch * cap, d), F32),
            pltpu.SemaphoreType.DMA((2,)),
        ],
        compiler_params=_params(("arbitrary", "arbitrary")),
        name="expert_ffn",
    )(*([idx3] * (2 * batch)), h3, w1, w3, w2)


def _combine_body(base_ref, pos_ref, gsel_ref, h_ref, y_ref, g_ref, b_ref, o_ref, win_ref, sem,
                  *, nb, tb, cap, win, wsm):
    bi = pl.program_id(0)
    j = pl.program_id(1)
    step = bi * nb + j
    nsteps = pl.num_programs(0) * nb
    slot = step % 2

    def w0_of(s, e, w):
        base = base_ref[s * N_EXPERTS + e]
        return pl.multiple_of(jnp.minimum((base // 16) * 16, cap - w), 16)

    def fits_small(s):
        ok = None
        for e in range(N_EXPERTS):
            nxt = base_ref[jnp.minimum(s + 1, nsteps - 1) * N_EXPERTS + e]
            end = jnp.where(s % nb == nb - 1, cap, nxt)
            ok_e = end <= w0_of(s, e, wsm) + wsm
            ok = ok_e if ok is None else ok & ok_e
        return ok

    def copies(s, sl, w):
        b_s = s // nb
        return [pltpu.make_async_copy(y_ref.at[b_s, e, pl.ds(w0_of(s, e, w), w), :],
                                      win_ref.at[sl, e, pl.ds(0, w), :], sem.at[sl]) for e in range(N_EXPERTS)]

    def start(s, sl):
        small = fits_small(s)

        @pl.when(small)
        def _():
            for cp in copies(s, sl, wsm):
                cp.start()

        @pl.when(jnp.logical_not(small))
        def _():
            for cp in copies(s, sl, win):
                cp.start()

    @pl.when(step == 0)
    def _():
        start(step, slot)

    @pl.when(step + 1 < nsteps)
    def _():
        start(step + 1, 1 - slot)

    def finish(w):
        for cp in copies(step, slot, w):
            cp.wait()
        p = pos_ref[...]
        gs = gsel_ref[...]
        lane = lax.broadcasted_iota(I32, (tb, w), 1)
        acc = jnp.zeros((tb, D_MODEL), F32)
        for e in range(N_EXPERTS):
            rel = p[:, e:e + 1] - w0_of(step, e, w)
            s_e = jnp.where(rel == lane, gs[:, e:e + 1], 0.0).astype(BF16)
            acc = acc + _dot(s_e, win_ref[slot, e, 0:w, :])
        r = DEEPNORM_ALPHA * h_ref[...] + acc
        o_ref[...] = _layer_norm(r, g_ref[...], b_ref[...])

    small_now = fits_small(step)

    @pl.when(small_now)
    def _():
        finish(wsm)

    @pl.when(jnp.logical_not(small_now))
    def _():
        finish(win)


def _combine(base_flat, pos, gsel, h, y, g, b, batch, seq, cap, tb, wsm=48):
    nb = seq // tb
    win = tb + 16
    return pl.pallas_call(
        functools.partial(_combine_body, nb=nb, tb=tb, cap=cap, win=win, wsm=wsm),
        grid_spec=pltpu.PrefetchScalarGridSpec(
            num_scalar_prefetch=1,
            grid=(batch, nb),
            in_specs=[
                pl.BlockSpec((tb, LANES), lambda bi, j, base: (bi * nb + j, 0)),
                pl.BlockSpec((tb, LANES), lambda bi, j, base: (bi * nb + j, 0)),
                pl.BlockSpec((tb, D_MODEL), lambda bi, j, base: (bi * nb + j, 0)),
                pl.BlockSpec(memory_space=pl.ANY),
                pl.BlockSpec((1, D_MODEL), lambda bi, j, base: (0, 0)),
                pl.BlockSpec((1, D_MODEL), lambda bi, j, base: (0, 0)),
            ],
            out_specs=pl.BlockSpec((tb, D_MODEL), lambda bi, j, base: (bi * nb + j, 0)),
            scratch_shapes=[pltpu.VMEM((2, N_EXPERTS, win, D_MODEL), BF16), pltpu.SemaphoreType.DMA((2,))],
        ),
        out_shape=jax.ShapeDtypeStruct((batch * seq, D_MODEL), F32),
        compiler_params=_params(("arbitrary", "arbitrary")),
        name="combine_ln2",
    )(base_flat, pos, gsel, h, y, g, b)


def _pad_lanes(v, width=LANES):
    return jnp.pad(v, [(0, 0)] * (v.ndim - 1) + [(0, width - v.shape[-1])])


def _hi_lo(w):
    hi = w.astype(BF16)
    lo = (w - hi.astype(F32)).astype(BF16)
    return jnp.concatenate([hi, lo], axis=1)


def _layer(x, w_in, conv_w, conv_b, ssd_a_log_f, ssd_a_log_b, ssd_dt_bias_f, ssd_dt_bias_b, ssd_d, ssd_norm_g,
           w_out_ssd, s5_a_re_f, s5_a_im_f, s5_log_step_f, s5_a_re_b, s5_a_im_b, s5_log_step_b, s5_b_re, s5_b_im,
           s5_c_re, s5_c_im, s5_d, w_glu_a, w_glu_b, w_o, ln1_g, ln1_b, w_router, w_e1, w_e3, w_e2, ln2_g, ln2_b):
    batch, seq, d = x.shape
    n = batch * seq
    x2 = x.reshape(n, d)
    c0, c1, c2, c3 = SSD_INNER, SSD_INNER + XBC_DIM, SSD_INNER + XBC_DIM + 2 * SSD_HEADS, \
        SSD_INNER + XBC_DIM + 2 * SSD_HEADS + S5_WIDTH
    wdt = w_in[:, c1:c2]
    wdt = jnp.concatenate([_pad_lanes(wdt[:, :SSD_HEADS]), _pad_lanes(wdt[:, SSD_HEADS:])], axis=1)
    z, xbc, dt, u, gates = _in_proj(x2, w_in[:, :c0].astype(BF16), w_in[:, c0:c1].astype(BF16), _hi_lo(wdt),
                                    w_in[:, c2:c3].astype(BF16), w_in[:, c3:].astype(BF16), tb=512)

    t = S5_CHUNK
    nc = seq // t
    nr = nc // S5_SEGS
    u2 = u.reshape(batch * nc, t * S5_WIDTH)
    mf = _s5_mats(s5_a_re_f, s5_a_im_f, s5_log_step_f, s5_b_re, s5_b_im, s5_c_re, s5_c_im, False)
    mb = _s5_mats(s5_a_re_b, s5_a_im_b, s5_log_step_b, s5_b_re, s5_b_im, s5_c_re, s5_c_im, True)
    toep = _tile_toep((mf[0] + mb[0]).astype(BF16))
    win = jnp.stack([_tile_in(m.astype(BF16)) for m in (mf[1], mf[2], mb[1], mb[2])], axis=1)
    wout = jnp.stack([_tile_out(m.astype(BF16)) for m in (mf[3], mf[4], mb[3], mb[4])], axis=1)
    sw = S5_GPT * S5_STATE

    def powc(lr, li, k):
        mag = jnp.exp(lr * k)
        return (mag * jnp.cos(li * k)).reshape(S5_TILES, sw), (mag * jnp.sin(li * k)).reshape(S5_TILES, sw)

    coef = jnp.stack([*powc(*mf[5], float(t)), *powc(*mf[5], float(t * nr)),
                      *powc(*mb[5], float(t)), *powc(*mb[5], float(t * nr))], axis=1)
    y3 = _s5(u2, toep, win, wout, coef, batch, nc)
    s5y = y3.transpose(1, 0, 2).reshape(n, S5_WIDTH)

    xs, bm, cm = _conv(xbc, conv_w, conv_b.reshape(1, XBC_DIM), seq, tb=512)

    def ssd_par(a_log, bias):
        p = jnp.zeros((8, LANES), F32)
        return p.at[0, :SSD_HEADS].set(-jnp.exp(a_log)).at[1, :SSD_HEADS].set(bias)

    yb = _ssd(xs, bm, cm, dt, ssd_par(ssd_a_log_b, ssd_dt_bias_b), batch, seq, True)
    yf = _ssd(xs, bm, cm, dt, ssd_par(ssd_a_log_f, ssd_dt_bias_f), batch, seq, False)

    wr = _hi_lo(_pad_lanes(w_router))
    h, lg = _mid(yf, yb, xs, z, s5y, u, gates, x2, jnp.repeat(ssd_d, SSD_HEADDIM).reshape(1, SSD_INNER),
                 ssd_norm_g.reshape(1, SSD_INNER), w_out_ssd.astype(BF16), s5_d.reshape(1, S5_WIDTH),
                 w_glu_a.astype(BF16), w_glu_b.astype(BF16), w_o.astype(BF16), ln1_g.reshape(1, d),
                 ln1_b.reshape(1, d), wr, tb=256)

    cap = CAPACITY_FACTOR * seq // N_EXPERTS
    tbk = 128
    gsel, pos, base = _topk(lg, batch, seq, cap, tbk)
    base_flat = base[:, :, :N_EXPERTS].reshape(-1)
    idx = _compact(base_flat, pos, batch, seq, cap, tbk)
    idx3 = idx[:, :, :cap // LANES, :].reshape(batch * N_EXPERTS, 1, cap)
    y = _ffn(idx3, h.reshape(n, d // LANES, LANES), w_e1, w_e3, w_e2, batch, seq, cap, ft=256)
    out = _combine(base_flat, pos, gsel, h, y, ln2_g.reshape(1, d), ln2_b.reshape(1, d), batch, seq, cap, tbk)
    return out.reshape(batch, seq, d)


def kernel(x, w_in, conv_w, conv_b, ssd_a_log_f, ssd_a_log_b, ssd_dt_bias_f, ssd_dt_bias_b, ssd_d, ssd_norm_g, w_out_ssd, s5_a_re_f, s5_a_im_f, s5_log_step_f, s5_a_re_b, s5_a_im_b, s5_log_step_b, s5_b_re, s5_b_im, s5_c_re, s5_c_im, s5_d, w_glu_a, w_glu_b, w_o, ln1_g, ln1_b, w_router, w_e1, w_e3, w_e2, ln2_g, ln2_b):
    h = x
    for i in range(DEPTH):
        h = _layer(h, w_in[i], conv_w[i], conv_b[i], ssd_a_log_f[i], ssd_a_log_b[i], ssd_dt_bias_f[i],
                   ssd_dt_bias_b[i], ssd_d[i], ssd_norm_g[i], w_out_ssd[i], s5_a_re_f[i], s5_a_im_f[i],
                   s5_log_step_f[i], s5_a_re_b[i], s5_a_im_b[i], s5_log_step_b[i], s5_b_re[i], s5_b_im[i],
                   s5_c_re[i], s5_c_im[i], s5_d[i], w_glu_a[i], w_glu_b[i], w_o[i], ln1_g[i], ln1_b[i],
                   w_router[i], w_e1[i], w_e3[i], w_e2[i], ln2_g[i], ln2_b[i])
    return h
```
